```python
import math
import jax, jax.numpy as jnp
from jax import lax
import numpy as np

D_MODEL = 2048
BATCH = 1
SEQ = 16384
DEPTH = 2

CONV_CH = 1024
CONV_K = 31
SSD_HEADS = 32
SSD_HEADDIM = 64
SSD_INNER = SSD_HEADS * SSD_HEADDIM
SSD_GROUPS = 4
SSD_STATE = 128
SSD_CONV_K = 4
SSD_CHUNK = 128
SSD_XBC = SSD_INNER + 2 * SSD_GROUPS * SSD_STATE
DA_HEADS = 8
DA_HEAD_DIM = 64
DA_V_DIM = 2 * DA_HEAD_DIM
DA_QK_WIDTH = DA_HEADS * 2 * DA_HEAD_DIM
DA_V_WIDTH = DA_HEADS * DA_V_DIM
ROPE_DIM = DA_HEAD_DIM // 4
ROPE_THETA = 500000.0
Q_BLOCK = 128
D_FF = 5632
FFN_CONV_K = 3
N_BRANCH = 3
RMS_EPS = 1e-6
LN_EPS = 1e-5
NEG_INF = -1e30

W_CONV_IN = 2 * CONV_CH
W_Z = SSD_INNER
W_DT = SSD_HEADS
W_GATES = N_BRANCH * D_MODEL
_O1 = W_CONV_IN
_O2 = _O1 + W_Z
_O3 = _O2 + SSD_XBC
_O4 = _O3 + W_DT
_O5 = _O4 + DA_QK_WIDTH
_O6 = _O5 + DA_QK_WIDTH
_O7 = _O6 + DA_V_WIDTH
IN_WIDTH = _O7 + W_GATES
IN_SPLITS = [_O1, _O2, _O3, _O4, _O5, _O6, _O7]

kernel_name = 'hybrid_gated_conformer_ssd_diffattn'


def rms_norm(x, w, eps=RMS_EPS):
    xf = x.astype(jnp.float32)
    y = xf * lax.rsqrt(jnp.mean(xf * xf, axis=-1, keepdims=True) + eps)
    return (y * w.astype(jnp.float32)).astype(x.dtype)


def layer_norm(x, w, b, eps=LN_EPS):
    xf = x.astype(jnp.float32)
    mu = jnp.mean(xf, axis=-1, keepdims=True)
    xc = xf - mu
    var = jnp.mean(xc * xc, axis=-1, keepdims=True)
    return (xc * lax.rsqrt(var + eps) * w.astype(jnp.float32) + b.astype(jnp.float32)).astype(x.dtype)


def causal_dwconv(x, w):
    k = w.shape[0]
    return lax.conv_general_dilated(
        x, w.astype(x.dtype)[:, None, :], window_strides=(1,), padding=[(k - 1, 0)],
        dimension_numbers=('NWC', 'WIO', 'NWC'), feature_group_count=x.shape[-1])


def rope_tables(positions):
    inv = 1.0 / (ROPE_THETA ** (jnp.arange(0, ROPE_DIM, 2, dtype=jnp.float32) / ROPE_DIM))
    ang = positions.astype(jnp.float32)[..., None] * inv
    return jnp.cos(ang), jnp.sin(ang)


def apply_partial_rope(t, cos, sin):
    half = ROPE_DIM // 2
    c = cos[:, :, None, None, :]
    s = sin[:, :, None, None, :]
    t1 = t[..., :half].astype(jnp.float32)
    t2 = t[..., half:ROPE_DIM].astype(jnp.float32)
    rot = jnp.concatenate([t1 * c - t2 * s, t2 * c + t1 * s], axis=-1).astype(t.dtype)
    return jnp.concatenate([rot, t[..., ROPE_DIM:]], axis=-1)


def conformer_conv_branch(u, w_dw, b_dw, ln_w, ln_b, w_out):
    a, g = jnp.split(u, 2, axis=-1)
    h = a * jax.nn.sigmoid(g)
    h = causal_dwconv(h, w_dw) + b_dw.astype(h.dtype)
    h = jax.nn.silu(layer_norm(h, ln_w, ln_b))
    return h @ w_out


def ssd_chunked_scan(x, dt, a, bm, cm):
    bsz, s_len, h, p = x.shape
    g, n = bm.shape[2], bm.shape[3]
    hg = h // g
    L = SSD_CHUNK
    nc = s_len // L
    xs = x.astype(jnp.float32).reshape(bsz, nc, L, g, hg, p).transpose(1, 0, 2, 3, 4, 5)
    dts = dt.reshape(bsz, nc, L, g, hg).transpose(1, 0, 2, 3, 4)
    bs = bm.astype(jnp.float32).reshape(bsz, nc, L, g, n).transpose(1, 0, 2, 3, 4)
    cs_ = cm.astype(jnp.float32).reshape(bsz, nc, L, g, n).transpose(1, 0, 2, 3, 4)
    ag = a.reshape(g, hg)
    tri = jnp.tril(jnp.ones((L, L), dtype=bool))[None, :, :, None, None]

    def step(state, inp):
        xc, dtc, bc, cc = inp
        cum = jnp.cumsum(dtc * ag, axis=1)
        seg = cum[:, :, None] - cum[:, None, :]
        decay = jnp.exp(jnp.where(tri, seg, -jnp.inf))
        cb = jnp.einsum('btgn,bsgn->btsg', cc, bc)
        wts = cb[..., None] * decay * dtc[:, None]
        y_diag = jnp.einsum('btsgh,bsghp->btghp', wts, xc)
        y_off = jnp.einsum('btgn,bghpn->btghp', cc, state) * jnp.exp(cum)[..., None]
        decay_end = jnp.exp(cum[:, -1:] - cum) * dtc
        new_state = state * jnp.exp(cum[:, -1])[..., None, None] + jnp.einsum(
            'bsgh,bsgn,bsghp->bghpn', decay_end, bc, xc)
        return new_state, y_diag + y_off

    state0 = jnp.zeros((bsz, g, hg, p, n), jnp.float32)
    _, ys = lax.scan(step, state0, (xs, dts, bs, cs_))
    return ys.transpose(1, 0, 2, 3, 4, 5).reshape(bsz, s_len, h, p)


def mamba2_branch(z, xbc, dt, conv_w, conv_b, dt_bias, a_log, d_skip, norm_w, w_out):
    bsz, s_len, _ = z.shape
    xbc = jax.nn.silu(causal_dwconv(xbc, conv_w) + conv_b.astype(xbc.dtype))
    xs, bm, cm = jnp.split(xbc, [SSD_INNER, SSD_INNER + SSD_GROUPS * SSD_STATE], axis=-1)
    xs = xs.reshape(bsz, s_len, SSD_HEADS, SSD_HEADDIM)
    bm = bm.reshape(bsz, s_len, SSD_GROUPS, SSD_STATE)
    cm = cm.reshape(bsz, s_len, SSD_GROUPS, SSD_STATE)
    dtf = jax.nn.softplus(dt.astype(jnp.float32) + dt_bias.astype(jnp.float32))
    a = -jnp.exp(a_log.astype(jnp.float32))
    y = ssd_chunked_scan(xs, dtf, a, bm, cm) + xs.astype(jnp.float32) * d_skip.astype(jnp.float32)[:, None]
    y = y.astype(z.dtype).reshape(bsz, s_len, SSD_INNER) * jax.nn.silu(z)
    gs = SSD_INNER // SSD_GROUPS
    y = rms_norm(y.reshape(bsz, s_len, SSD_GROUPS, gs), norm_w.reshape(SSD_GROUPS, gs))
    return y.reshape(bsz, s_len, SSD_INNER) @ w_out


def diff_attn_branch(q, k, v, cos, sin, lq1, lk1, lq2, lk2, subln_w, w_out, lambda_init):
    bsz, s_len, _ = q.shape
    q = apply_partial_rope(q.reshape(bsz, s_len, DA_HEADS, 2, DA_HEAD_DIM), cos, sin)
    k = apply_partial_rope(k.reshape(bsz, s_len, DA_HEADS, 2, DA_HEAD_DIM), cos, sin)
    v = v.reshape(bsz, s_len, DA_HEADS, DA_V_DIM)
    lam = (jnp.exp(jnp.sum(lq1.astype(jnp.float32) * lk1.astype(jnp.float32)))
           - jnp.exp(jnp.sum(lq2.astype(jnp.float32) * lk2.astype(jnp.float32))) + lambda_init)
    scale = DA_HEAD_DIM ** -0.5
    nblk = s_len // Q_BLOCK
    qb = q.reshape(bsz, nblk, Q_BLOCK, DA_HEADS, 2, DA_HEAD_DIM).transpose(1, 0, 2, 3, 4, 5)
    kpos = jnp.arange(s_len)

    def block(args):
        q_blk, i = args
        sc = jnp.einsum('bqhcd,bkhcd->bhcqk', q_blk, k).astype(jnp.float32) * scale
        qpos = i * Q_BLOCK + jnp.arange(Q_BLOCK)
        mask = kpos[None, :] <= qpos[:, None]
        prob = jax.nn.softmax(jnp.where(mask, sc, NEG_INF), axis=-1)
        diff = prob[:, :, 0] - lam * prob[:, :, 1]
        return jnp.einsum('bhqk,bkhd->bqhd', diff.astype(v.dtype), v)

    o = lax.map(block, (qb, jnp.arange(nblk)))
    o = o.transpose(1, 0, 2, 3, 4).reshape(bsz, s_len, DA_HEADS, DA_V_DIM)
    o = rms_norm(o, subln_w, eps=LN_EPS) * (1.0 - lambda_init)
    return o.reshape(bsz, s_len, DA_V_WIDTH) @ w_out


def conv_glu_ffn(h, w_up, w_dw, w_down):
    gate, up = jnp.split(h @ w_up, 2, axis=-1)
    gate = causal_dwconv(gate, w_dw)
    return (jax.nn.silu(gate) * up) @ w_down


def _normal(k, shape, scale):
    return jax.random.normal(k, shape, jnp.float32) * scale


def setup_inputs(seed: int = 0) -> dict:
    key = jax.random.key(seed)
    ks = jax.random.split(key, 32)
    L = DEPTH
    dt0 = jnp.exp(jax.random.uniform(ks[11], (L, SSD_HEADS), jnp.float32, math.log(1e-3), math.log(1e-1)))
    return {
        'x': _normal(ks[0], (BATCH, SEQ, D_MODEL), 1.0),
        'positions': jnp.broadcast_to(jnp.arange(SEQ, dtype=jnp.int32), (BATCH, SEQ)),
        'norm1_w': 1.0 + _normal(ks[1], (L, D_MODEL), 0.02),
        'w_in': _normal(ks[2], (L, D_MODEL, IN_WIDTH), D_MODEL ** -0.5),
        'gate_b': _normal(ks[3], (L, N_BRANCH, D_MODEL), 0.01),
        'conv_dw_w': _normal(ks[4], (L, CONV_K, CONV_CH), CONV_K ** -0.5),
        'conv_dw_b': _normal(ks[5], (L, CONV_CH), 0.01),
        'conv_ln_w': 1.0 + _normal(ks[6], (L, CONV_CH), 0.02),
        'conv_ln_b': _normal(ks[7], (L, CONV_CH), 0.01),
        'conv_out_w': _normal(ks[8], (L, CONV_CH, D_MODEL), CONV_CH ** -0.5),
        'ssd_conv_w': _normal(ks[9], (L, SSD_CONV_K, SSD_XBC), SSD_CONV_K ** -0.5),
        'ssd_conv_b': _normal(ks[10], (L, SSD_XBC), 0.01),
        'ssd_dt_bias': dt0 + jnp.log(-jnp.expm1(-dt0)),
        'ssd_a_log': jnp.log(jax.random.uniform(ks[12], (L, SSD_HEADS), jnp.float32, 1.0, 16.0)),
        'ssd_d': 1.0 + _normal(ks[13], (L, SSD_HEADS), 0.02),
        'ssd_norm_w': 1.0 + _normal(ks[14], (L, SSD_INNER), 0.02),
        'ssd_out_w': _normal(ks[15], (L, SSD_INNER, D_MODEL), SSD_INNER ** -0.5),
        'da_lambda_q1': _normal(ks[16], (L, DA_HEAD_DIM), 0.1),
        'da_lambda_k1': _normal(ks[17], (L, DA_HEAD_DIM), 0.1),
        'da_lambda_q2': _normal(ks[18], (L, DA_HEAD_DIM), 0.1),
        'da_lambda_k2': _normal(ks[19], (L, DA_HEAD_DIM), 0.1),
        'da_subln_w': 1.0 + _normal(ks[20], (L, DA_V_DIM), 0.02),
        'da_out_w': _normal(ks[21], (L, DA_V_WIDTH, D_MODEL), DA_V_WIDTH ** -0.5),
        'w_o': _normal(ks[22], (L, D_MODEL, D_MODEL), D_MODEL ** -0.5),
        'norm2_w': 1.0 + _normal(ks[23], (L, D_MODEL), 0.02),
        'ffn_up_w': _normal(ks[24], (L, D_MODEL, 2 * D_FF), D_MODEL ** -0.5),
        'ffn_dw_w': _normal(ks[25], (L, FFN_CONV_K, D_FF), FFN_CONV_K ** -0.5),
        'ffn_down_w': _normal(ks[26], (L, D_FF, D_MODEL), D_FF ** -0.5),
        'final_norm_w': 1.0 + _normal(ks[27], (D_MODEL,), 0.02),
    }


def reference(x, positions, norm1_w, w_in, gate_b, conv_dw_w, conv_dw_b, conv_ln_w, conv_ln_b, conv_out_w,
              ssd_conv_w, ssd_conv_b, ssd_dt_bias, ssd_a_log, ssd_d, ssd_norm_w, ssd_out_w,
              da_lambda_q1, da_lambda_k1, da_lambda_q2, da_lambda_k2, da_subln_w, da_out_w,
              w_o, norm2_w, ffn_up_w, ffn_dw_w, ffn_down_w, final_norm_w):
    bsz, s_len, _ = x.shape
    cos, sin = rope_tables(positions)
    for l in range(DEPTH):
        lambda_init = 0.8 - 0.6 * math.exp(-0.3 * l)
        xn = rms_norm(x, norm1_w[l])
        u = xn @ w_in[l]
        u_conv, u_z, u_xbc, u_dt, u_q, u_k, u_v, u_gate = jnp.split(u, IN_SPLITS, axis=-1)
        y_a = conformer_conv_branch(u_conv, conv_dw_w[l], conv_dw_b[l], conv_ln_w[l], conv_ln_b[l], conv_out_w[l])
        y_b = mamba2_branch(u_z, u_xbc, u_dt, ssd_conv_w[l], ssd_conv_b[l], ssd_dt_bias[l], ssd_a_log[l],
                            ssd_d[l], ssd_norm_w[l], ssd_out_w[l])
        y_c = diff_attn_branch(u_q, u_k, u_v, cos, sin, da_lambda_q1[l], da_lambda_k1[l], da_lambda_q2[l],
                               da_lambda_k2[l], da_subln_w[l], da_out_w[l], lambda_init)
        gates = jax.nn.sigmoid((u_gate.reshape(bsz, s_len, N_BRANCH, D_MODEL)
                                + gate_b[l].astype(u_gate.dtype)).astype(jnp.float32)).astype(x.dtype)
        merged = gates[:, :, 0] * y_a + gates[:, :, 1] * y_b + gates[:, :, 2] * y_c
        x = x + merged @ w_o[l]
        x = x + conv_glu_ffn(rms_norm(x, norm2_w[l]), ffn_up_w[l], ffn_dw_w[l], ffn_down_w[l])
    return rms_norm(x, final_norm_w)
```

```python
import functools
import math

import jax
import jax.numpy as jnp
from jax import lax
from jax.experimental import pallas as pl
from jax.experimental.pallas import tpu as pltpu

F32 = jnp.float32
BF16 = jnp.bfloat16

D_MODEL = 2048
DEPTH = 2
CONV_CH = 1024
CONV_K = 31
SSD_HEADS = 32
SSD_HEADDIM = 64
SSD_INNER = SSD_HEADS * SSD_HEADDIM
SSD_GROUPS = 4
SSD_STATE = 128
SSD_CONV_K = 4
SSD_CHUNK = 128
SSD_GROUP_W = SSD_INNER // SSD_GROUPS
DA_HEADS = 8
DA_HEAD_DIM = 64
DA_V_DIM = 128
ROPE_DIM = 16
ROPE_THETA = 500000.0
D_FF = 5632
FFN_CONV_K = 3
RMS_EPS = 1e-6
LN_EPS = 1e-5
NEG_INF = -1e30

C_CONV_A = 0
C_CONV_G = 1024
C_Z = 2048
C_X = 4096
C_B = 6144
C_C = 6656
C_Q = 7168
C_K = 8192
C_V = 9216
C_GATE = 10240
U_WIDTH = 16384
DT_OFF = 7168
DT_PAD = 128

LANES = 128
VMEM_LIMIT = 56 * 1024 * 1024


def _cparams(sem):
    return pltpu.CompilerParams(dimension_semantics=sem, vmem_limit_bytes=VMEM_LIMIT)


def _norm_matmul_body(*refs, has_side, row_chunk):
    if has_side:
        x_ref, nw_ref, w_ref, ws_ref, o_ref, os_ref, xn_ref = refs
    else:
        x_ref, nw_ref, w_ref, o_ref, xn_ref = refs
    tm = x_ref.shape[0]

    @pl.when(pl.program_id(1) == 0)
    def _():
        def chunk(c, carry):
            r0 = pl.multiple_of(c * row_chunk, row_chunk)
            x = x_ref[pl.ds(r0, row_chunk), :]
            ms = jnp.mean(x * x, axis=-1, keepdims=True)
            xn_ref[pl.ds(r0, row_chunk), :] = (x * lax.rsqrt(ms + RMS_EPS) * nw_ref[...]).astype(BF16)
            return carry
        lax.fori_loop(0, tm // row_chunk, chunk, 0)
        if has_side:
            os_ref[...] = jnp.dot(xn_ref[...], ws_ref[...], preferred_element_type=F32)

    o_ref[...] = jnp.dot(xn_ref[...], w_ref[...], preferred_element_type=F32).astype(o_ref.dtype)


def norm_matmul(x, nw, w, w_side=None, *, tm=1024, tn=1024):
    s, d = x.shape
    n = w.shape[1]
    tm = min(tm, s)
    tn = min(tn, n)
    has_side = w_side is not None
    in_specs = [pl.BlockSpec((tm, d), lambda i, j: (i, 0)),
                pl.BlockSpec((1, d), lambda i, j: (0, 0)),
                pl.BlockSpec((d, tn), lambda i, j: (0, j))]
    out_shape = [jax.ShapeDtypeStruct((s, n), BF16)]
    out_specs = [pl.BlockSpec((tm, tn), lambda i, j: (i, j))]
    args = [x, nw.reshape(1, d), w]
    if has_side:
        ns = w_side.shape[1]
        in_specs.append(pl.BlockSpec((d, ns), lambda i, j: (0, 0)))
        out_shape.append(jax.ShapeDtypeStruct((s, ns), F32))
        out_specs.append(pl.BlockSpec((tm, ns), lambda i, j: (i, 0)))
        args.append(w_side)
    res = pl.pallas_call(
        functools.partial(_norm_matmul_body, has_side=has_side, row_chunk=min(128, tm)),
        grid=(s // tm, n // tn),
        in_specs=in_specs, out_specs=out_specs, out_shape=out_shape,
        scratch_shapes=[pltpu.VMEM((tm, d), BF16)],
        compiler_params=_cparams(("arbitrary", "arbitrary")),
    )(*args)
    return res if has_side else res[0]


CONV_HALO = 32


def _conformer_body(a_ref, g_ref, ah_ref, gh_ref, w_ref, b_ref, lnw_ref, lnb_ref, o_ref, hbuf, cbuf):
    ts = a_ref.shape[0]
    i = pl.program_id(0)
    halo = ah_ref[...].astype(F32) * jax.nn.sigmoid(gh_ref[...].astype(F32))
    hbuf[0:CONV_HALO, :] = jnp.where(i > 0, halo, 0.0)
    hbuf[CONV_HALO:, :] = a_ref[...].astype(F32) * jax.nn.sigmoid(g_ref[...].astype(F32))
    base = CONV_HALO - (CONV_K - 1)
    for c in range(CONV_CH // LANES):
        cs = slice(c * LANES, (c + 1) * LANES)
        acc = jnp.zeros((ts, LANES), F32)
        for k in range(CONV_K):
            acc = acc + w_ref[k:k + 1, cs] * hbuf[base + k: base + k + ts, cs]
        cbuf[:, cs] = acc + b_ref[:, cs]
    y = cbuf[...]
    mu = jnp.mean(y, axis=-1, keepdims=True)
    yc = y - mu
    var = jnp.mean(yc * yc, axis=-1, keepdims=True)
    yn = yc * lax.rsqrt(var + LN_EPS) * lnw_ref[...] + lnb_ref[...]
    o_ref[...] = (yn * jax.nn.sigmoid(yn)).astype(o_ref.dtype)


def conformer_pre(u, w_dw, b_dw, ln_w, ln_b, *, ts=256):
    s = u.shape[0]
    ts = min(ts, s)
    hb = ts // CONV_HALO
    w_pad = jnp.zeros((32, CONV_CH), F32).at[:CONV_K].set(w_dw)
    row = lambda v: v.reshape(1, CONV_CH)
    halo_idx = lambda i: jnp.maximum(i * hb - 1, 0)
    return pl.pallas_call(
        _conformer_body,
        grid=(s // ts,),
        in_specs=[pl.BlockSpec((ts, CONV_CH), lambda i: (i, C_CONV_A // CONV_CH)),
                  pl.BlockSpec((ts, CONV_CH), lambda i: (i, C_CONV_G // CONV_CH)),
                  pl.BlockSpec((CONV_HALO, CONV_CH), lambda i: (halo_idx(i), C_CONV_A // CONV_CH)),
                  pl.BlockSpec((CONV_HALO, CONV_CH), lambda i: (halo_idx(i), C_CONV_G // CONV_CH)),
                  pl.BlockSpec((32, CONV_CH), lambda i: (0, 0)),
                  pl.BlockSpec((1, CONV_CH), lambda i: (0, 0)),
                  pl.BlockSpec((1, CONV_CH), lambda i: (0, 0)),
                  pl.BlockSpec((1, CONV_CH), lambda i: (0, 0))],
        out_specs=pl.BlockSpec((ts, CONV_CH), lambda i: (i, 0)),
        out_shape=jax.ShapeDtypeStruct((s, CONV_CH), BF16),
        scratch_shapes=[pltpu.VMEM((CONV_HALO + ts, CONV_CH), F32), pltpu.VMEM((ts, CONV_CH), F32)],
        compiler_params=_cparams(("arbitrary",)),
    )(u, u, u, u, w_pad, row(b_dw), row(ln_w), row(ln_b))


SSD_HALO = 8


def _split3(v):
    hi = v.astype(BF16)
    r1 = v - hi.astype(F32)
    mid = r1.astype(BF16)
    lo = (r1 - mid.astype(F32)).astype(BF16)
    return hi, mid, lo


def _dot01_left(m01, v):
    hi, mid, lo = _split3(v)
    d = lambda p: jnp.dot(m01, p, preferred_element_type=F32)
    return d(hi) + d(mid) + d(lo)


def _dot01_right(v, m01):
    hi, mid, lo = _split3(v)
    d = lambda p: jnp.dot(p, m01, preferred_element_type=F32)
    return d(hi) + d(mid) + d(lo)


def _softplus(x):
    return jnp.maximum(x, 0.0) + jnp.log1p(jnp.exp(-jnp.abs(x)))


def _silu(x):
    return x * jax.nn.sigmoid(x)


def _ssd_body(x_ref, b_ref, c_ref, z_ref, xh_ref, bh_ref, ch_ref, dt_ref,
              cwx_ref, cwb_ref, cwc_ref, cbx_ref, cbb_ref, cbc_ref,
              dtb_ref, alog_ref, dx_ref, nw_ref, ex_ref,
              o_ref, state, ebuf, xs_c, b_c, c_c, ybuf):
    ts = x_ref.shape[0]
    L = SSD_CHUNK
    i = pl.program_id(0)

    @pl.when(i == 0)
    def _():
        state[...] = jnp.zeros_like(state)

    def conv(in_ref, halo_ref, w_ref, bias_ref, out_ref):
        width = in_ref.shape[1]
        ebuf[0:SSD_HALO, 0:width] = jnp.where(i > 0, halo_ref[...].astype(F32), 0.0)
        ebuf[SSD_HALO:, 0:width] = in_ref[...].astype(F32)
        base = SSD_HALO - (SSD_CONV_K - 1)
        for c in range(width // 512):
            cs = slice(c * 512, (c + 1) * 512)
            acc = jnp.zeros((ts, 512), F32)
            for k in range(SSD_CONV_K):
                acc = acc + w_ref[k:k + 1, cs] * ebuf[base + k: base + k + ts, cs]
            out_ref[:, cs] = _silu(acc + bias_ref[:, cs])

    conv(x_ref, xh_ref, cwx_ref, cbx_ref, xs_c)
    conv(b_ref, bh_ref, cwb_ref, cbb_ref, b_c)
    conv(c_ref, ch_ref, cwc_ref, cbc_ref, c_c)

    a_row = -jnp.exp(alog_ref[...])
    rr = lax.broadcasted_iota(jnp.int32, (L, L), 0)
    cc = lax.broadcasted_iota(jnp.int32, (L, L), 1)
    tri = rr >= cc
    tri01 = jnp.where(tri, 1.0, 0.0).astype(BF16)
    lane_lo = lax.broadcasted_iota(jnp.int32, (L, LANES), 1) < SSD_HEADDIM
    ex = ex_ref[...]

    def chunk(ci, carry):
        r0 = pl.multiple_of(ci * L, L)
        rows = pl.ds(r0, L)
        dt = _softplus(dt_ref[rows, :] + dtb_ref[...])
        da = dt * a_row
        cum = _dot01_left(tri01, da)
        cum_t = cum.T
        dt_t = dt.T
        cum_x = _dot01_right(cum, ex)
        dt_x = _dot01_right(dt, ex)
        cum_last = cum_x[L - 1:L, :]
        e_x = jnp.exp(cum_x)
        xs = xs_c[rows, :]
        xd = (xs * jnp.exp(cum_last - cum_x) * dt_x).astype(BF16)
        e_last = jnp.exp(cum_last)
        for g in range(SSD_GROUPS):
            gs = slice(g * SSD_GROUP_W, (g + 1) * SSD_GROUP_W)
            ns = slice(g * SSD_STATE, (g + 1) * SSD_STATE)
            bg = b_c[rows, ns]
            cg = c_c[rows, ns].astype(BF16)
            cb = lax.dot_general(cg, bg.astype(BF16), (((1,), (1,)), ((), ())),
                                 preferred_element_type=F32)
            st = state[:, gs]
            y_off = jnp.dot(cg, st.astype(BF16), preferred_element_type=F32) * e_x[:, gs]
            state[:, gs] = e_last[:, gs] * st + jnp.dot(bg.T.astype(BF16), xd[:, gs],
                                                        preferred_element_type=F32)
            for pr in range(SSD_GROUP_W // LANES):
                h0 = g * (SSD_HEADS // SSD_GROUPS) + 2 * pr
                ps = slice(h0 * SSD_HEADDIM, h0 * SSD_HEADDIM + LANES)
                x_pair = xs[:, ps].astype(BF16)
                ys = []
                for h in (h0, h0 + 1):
                    seg = cum[:, h:h + 1] - cum_t[h:h + 1, :]
                    decay = jnp.where(tri, jnp.exp(jnp.minimum(seg, 0.0)), 0.0)
                    wts = (cb * decay * dt_t[h:h + 1, :]).astype(BF16)
                    ys.append(jnp.dot(wts, x_pair, preferred_element_type=F32))
                y_pair = jnp.where(lane_lo, ys[0], ys[1])
                ybuf[:, ps] = y_pair + y_off[:, pr * LANES:(pr + 1) * LANES]
        y = ybuf[...] + xs * dx_ref[...]
        y = y * _silu(z_ref[rows, :].astype(F32))
        for g in range(SSD_GROUPS):
            gs = slice(g * SSD_GROUP_W, (g + 1) * SSD_GROUP_W)
            yg = y[:, gs]
            ms = jnp.mean(yg * yg, axis=-1, keepdims=True)
            o_ref[rows, gs] = (yg * lax.rsqrt(ms + RMS_EPS) * nw_ref[:, gs]).astype(o_ref.dtype)
        return carry

    lax.fori_loop(0, ts // L, chunk, 0)


def ssd_pre(u, dt_raw, conv_w, conv_b, dt_bias, a_log, d_skip, norm_w, *, ts=256):
    s = u.shape[0]
    ts = min(ts, s)
    hb = ts // SSD_HALO
    halo_idx = lambda i: jnp.maximum(i * hb - 1, 0)
    cw = jnp.zeros((8, conv_w.shape[1]), F32).at[:SSD_CONV_K].set(conv_w)
    n_bc = SSD_GROUPS * SSD_STATE
    cwx, cwb, cwc = cw[:, :SSD_INNER], cw[:, SSD_INNER:SSD_INNER + n_bc], cw[:, SSD_INNER + n_bc:]
    cb2 = conv_b.reshape(1, -1)
    cbx, cbb, cbc = cb2[:, :SSD_INNER], cb2[:, SSD_INNER:SSD_INNER + n_bc], cb2[:, SSD_INNER + n_bc:]
    pad_h = lambda v: jnp.zeros((1, DT_PAD), F32).at[0, :SSD_HEADS].set(v)
    dx = jnp.repeat(d_skip, SSD_HEADDIM).reshape(1, SSD_INNER)
    ex = (jnp.arange(SSD_INNER)[None, :] // SSD_HEADDIM == jnp.arange(DT_PAD)[:, None]).astype(BF16)
    const = lambda shape: pl.BlockSpec(shape, lambda i: (0, 0))
    return pl.pallas_call(
        _ssd_body,
        grid=(s // ts,),
        in_specs=[pl.BlockSpec((ts, SSD_INNER), lambda i: (i, C_X // SSD_INNER)),
                  pl.BlockSpec((ts, n_bc), lambda i: (i, C_B // n_bc)),
                  pl.BlockSpec((ts, n_bc), lambda i: (i, C_C // n_bc)),
                  pl.BlockSpec((ts, SSD_INNER), lambda i: (i, C_Z // SSD_INNER)),
                  pl.BlockSpec((SSD_HALO, SSD_INNER), lambda i: (halo_idx(i), C_X // SSD_INNER)),
                  pl.BlockSpec((SSD_HALO, n_bc), lambda i: (halo_idx(i), C_B // n_bc)),
                  pl.BlockSpec((SSD_HALO, n_bc), lambda i: (halo_idx(i), C_C // n_bc)),
                  pl.BlockSpec((ts, DT_PAD), lambda i: (i, 0)),
                  const((8, SSD_INNER)), const((8, n_bc)), const((8, n_bc)),
                  const((1, SSD_INNER)), const((1, n_bc)), const((1, n_bc)),
                  const((1, DT_PAD)), const((1, DT_PAD)), const((1, SSD_INNER)), const((1, SSD_INNER)),
                  const((DT_PAD, SSD_INNER))],
        out_specs=pl.BlockSpec((ts, SSD_INNER), lambda i: (i, 0)),
        out_shape=jax.ShapeDtypeStruct((s, SSD_INNER), BF16),
        scratch_shapes=[pltpu.VMEM((SSD_STATE, SSD_INNER), F32),
                        pltpu.VMEM((SSD_HALO + ts, SSD_INNER), F32),
                        pltpu.VMEM((ts, SSD_INNER), F32),
                        pltpu.VMEM((ts, n_bc), F32),
                        pltpu.VMEM((ts, n_bc), F32),
                        pltpu.VMEM((SSD_CHUNK, SSD_INNER), F32)],
        compiler_params=_cparams(("arbitrary",)),
    )(u, u, u, u, u, u, u, dt_raw, cwx, cwb, cwc, cbx, cbb, cbc,
      pad_h(dt_bias), pad_h(a_log), dx, norm_w.reshape(1, SSD_INNER), ex)


def _rope_body(q_ref, k_ref, pos_ref, inv_ref, qo_ref, ko_ref):
    ts = q_ref.shape[0]
    lane = lax.broadcasted_iota(jnp.int32, (ts, LANES), 1) % DA_HEAD_DIM
    half = ROPE_DIM // 2
    ang = pos_ref[...].astype(F32) * inv_ref[...]
    cos = jnp.cos(ang)
    sin = jnp.sin(ang)
    c_all = jnp.where(lane < ROPE_DIM, cos, 1.0)
    s_lo = jnp.where(lane < half, -sin, 0.0)
    s_hi = jnp.where((lane >= half) & (lane < ROPE_DIM), sin, 0.0)
    scale = DA_HEAD_DIM ** -0.5
    for src, dst, mul in ((q_ref, qo_ref, scale), (k_ref, ko_ref, 1.0)):
        for c in range(src.shape[1] // LANES):
            cs = slice(c * LANES, (c + 1) * LANES)
            t = src[:, cs].astype(F32)
            up = pltpu.roll(t, LANES - half, axis=1)
            dn = pltpu.roll(t, half, axis=1)
            dst[:, cs] = ((t * c_all + up * s_lo + dn * s_hi) * mul).astype(dst.dtype)


def rope_prep(u, positions, *, ts=512):
    s = u.shape[0]
    ts = min(ts, s)
    w = DA_HEADS * 2 * DA_HEAD_DIM
    inv = 1.0 / (ROPE_THETA ** (jnp.arange(0, ROPE_DIM, 2, dtype=F32) / ROPE_DIM))
    lane = jnp.arange(LANES) % DA_HEAD_DIM
    inv_lane = jnp.where(lane < ROPE_DIM, inv[lane % (ROPE_DIM // 2)], 0.0).reshape(1, LANES)
    return pl.pallas_call(
        _rope_body,
        grid=(s // ts,),
        in_specs=[pl.BlockSpec((ts, w), lambda i: (i, C_Q // w)),
                  pl.BlockSpec((ts, w), lambda i: (i, C_K // w)),
                  pl.BlockSpec((ts, 1), lambda i: (i, 0)),
                  pl.BlockSpec((1, LANES), lambda i: (0, 0))],
        out_specs=[pl.BlockSpec((ts, w), lambda i: (i, 0)), pl.BlockSpec((ts, w), lambda i: (i, 0))],
        out_shape=[jax.ShapeDtypeStruct((s, w), BF16), jax.ShapeDtypeStruct((s, w), BF16)],
        compiler_params=_cparams(("arbitrary",)),
    )(u, u, positions.reshape(s, 1), inv_lane)


def _attn_body(q_ref, k_ref, v_ref, lq1_ref, lk1_ref, lq2_ref, lk2_ref, sw_ref, o_ref,
               m_ref, l_ref, acc_ref, *, lambda_init, tk):
    tq = q_ref.shape[0]
    i = pl.program_id(1)
    lane = lax.broadcasted_iota(jnp.int32, (tq, LANES), 1)
    q = q_ref[...]
    zero = jnp.zeros_like(q)
    qc = (jnp.where(lane < DA_HEAD_DIM, q, zero), jnp.where(lane >= DA_HEAD_DIM, q, zero))
    m_ref[...] = jnp.full(m_ref.shape, NEG_INF, F32)
    l_ref[...] = jnp.zeros(l_ref.shape, F32)
    acc_ref[...] = jnp.zeros(acc_ref.shape, F32)

    def step(j, masked):
        r0 = pl.multiple_of(j * tk, tk)
        kb = k_ref[pl.ds(r0, tk), :]
        vb = v_ref[pl.ds(r0, tk), :]
        for c in range(2):
            s = lax.dot_general(qc[c], kb, (((1,), (1,)), ((), ())), preferred_element_type=F32)
            if masked:
                rr = lax.broadcasted_iota(jnp.int32, (tq, tk), 0)
                cc = lax.broadcasted_iota(jnp.int32, (tq, tk), 1)
                s = jnp.where(cc <= rr, s, NEG_INF)
            m_prev = m_ref[c]
            m_new = jnp.maximum(m_prev, jnp.max(s, axis=-1, keepdims=True))
            alpha = jnp.exp(m_prev - m_new)
            p = jnp.exp(s - m_new)
            l_ref[c] = alpha * l_ref[c] + jnp.sum(p, axis=-1, keepdims=True)
            acc_ref[c] = alpha * acc_ref[c] + jnp.dot(p.astype(BF16), vb, preferred_element_type=F32)
            m_ref[c] = m_new

    def body(j, carry):
        step(j, False)
        return carry

    lax.fori_loop(0, i, body, 0)
    step(i, True)

    lam = (jnp.exp(jnp.sum(lq1_ref[...] * lk1_ref[...], axis=-1, keepdims=True))
           - jnp.exp(jnp.sum(lq2_ref[...] * lk2_ref[...], axis=-1, keepdims=True)) + lambda_init)
    o = acc_ref[0] / l_ref[0] - lam * (acc_ref[1] / l_ref[1])
    ms = jnp.mean(o * o, axis=-1, keepdims=True)
    o = o * lax.rsqrt(ms + LN_EPS) * sw_ref[...] * (1.0 - lambda_init)
    o_ref[...] = o.astype(o_ref.dtype)


def diff_attention(q_r, k_r, u, lq1, lk1, lq2, lk2, subln_w, lambda_init, *, tq=512):
    s = q_r.shape[0]
    tq = min(tq, s)
    row = lambda v: v.reshape(1, -1)
    const = lambda n: pl.BlockSpec((1, n), lambda h, i: (0, 0))
    return pl.pallas_call(
        functools.partial(_attn_body, lambda_init=lambda_init, tk=tq),
        grid=(DA_HEADS, s // tq),
        in_specs=[pl.BlockSpec((tq, LANES), lambda h, i: (i, h)),
                  pl.BlockSpec((s, LANES), lambda h, i: (0, h)),
                  pl.BlockSpec((s, LANES), lambda h, i: (0, C_V // LANES + h)),
                  const(DA_HEAD_DIM), const(DA_HEAD_DIM), const(DA_HEAD_DIM), const(DA_HEAD_DIM),
                  const(DA_V_DIM)],
        out_specs=pl.BlockSpec((tq, LANES), lambda h, i: (i, h)),
        out_shape=jax.ShapeDtypeStruct((s, DA_HEADS * DA_V_DIM), BF16),
        scratch_shapes=[pltpu.VMEM((2, tq, 1), F32), pltpu.VMEM((2, tq, 1), F32),
                        pltpu.VMEM((2, tq, DA_V_DIM), F32)],
        compiler_params=_cparams(("arbitrary", "arbitrary")),
    )(q_r, k_r, u, row(lq1), row(lk1), row(lq2), row(lk2), row(subln_w))


def _merge_body(x_ref, ha_ref, hb_ref, hc_ref, wa_ref, wb_ref, wc_ref,
                ga_ref, gb_ref, gc_ref, bias_ref, wo_ref, o_ref):
    j = pl.program_id(1)

    def gated(h_ref, w_ref, g_ref, bi):
        y = jnp.dot(h_ref[...], w_ref[...], preferred_element_type=F32)
        return jax.nn.sigmoid(g_ref[...].astype(F32) + bias_ref[bi:bi + 1, :]) * y

    m = gated(ha_ref, wa_ref, ga_ref, 0) + gated(hb_ref, wb_ref, gb_ref, 1) + gated(hc_ref, wc_ref, gc_ref, 2)
    contrib = jnp.dot(m.astype(BF16), wo_ref[...], preferred_element_type=F32)

    @pl.when(j == 0)
    def _():
        o_ref[...] = x_ref[...] + contrib

    @pl.when(j > 0)
    def _():
        o_ref[...] += contrib


def merge_out(x, h_a, h_b, h_c, w_a, w_b, w_c, u, gate_b, w_o, *, tm=512, tn=512):
    s, d = x.shape
    tm = min(tm, s)
    gblk = C_GATE // tn
    nblk = d // tn
    return pl.pallas_call(
        _merge_body,
        grid=(s // tm, d // tn),
        in_specs=[pl.BlockSpec((tm, d), lambda i, j: (i, 0)),
                  pl.BlockSpec((tm, h_a.shape[1]), lambda i, j: (i, 0)),
                  pl.BlockSpec((tm, h_b.shape[1]), lambda i, j: (i, 0)),
                  pl.BlockSpec((tm, h_c.shape[1]), lambda i, j: (i, 0)),
                  pl.BlockSpec((w_a.shape[0], tn), lambda i, j: (0, j)),
                  pl.BlockSpec((w_b.shape[0], tn), lambda i, j: (0, j)),
                  pl.BlockSpec((w_c.shape[0], tn), lambda i, j: (0, j)),
                  pl.BlockSpec((tm, tn), lambda i, j: (i, gblk + j)),
                  pl.BlockSpec((tm, tn), lambda i, j: (i, gblk + nblk + j)),
                  pl.BlockSpec((tm, tn), lambda i, j: (i, gblk + 2 * nblk + j)),
                  pl.BlockSpec((3, tn), lambda i, j: (0, j)),
                  pl.BlockSpec((tn, d), lambda i, j: (j, 0))],
        out_specs=pl.BlockSpec((tm, d), lambda i, j: (i, 0)),
        out_shape=jax.ShapeDtypeStruct((s, d), F32),
        compiler_params=_cparams(("arbitrary", "arbitrary")),
    )(x, h_a, h_b, h_c, w_a, w_b, w_c, u, u, u, gate_b, w_o)


FFN_HALO = 8


def _ffn_down_body(x_ref, g_ref, gh_ref, up_ref, cw_ref, wd_ref, fw_ref, o_ref, ebuf, *, final_norm):
    tm = g_ref.shape[0]
    i = pl.program_id(0)
    k = pl.program_id(1)
    ebuf[0:FFN_HALO, :] = jnp.where(i > 0, gh_ref[...].astype(F32), 0.0)
    ebuf[FFN_HALO:, :] = g_ref[...].astype(F32)
    base = FFN_HALO - (FFN_CONV_K - 1)
    acc = jnp.zeros(g_ref.shape, F32)
    for t in range(FFN_CONV_K):
        acc = acc + cw_ref[t:t + 1, :] * ebuf[base + t: base + t + tm, :]
    act = (_silu(acc) * up_ref[...].astype(F32)).astype(BF16)
    contrib = jnp.dot(act, wd_ref[...], preferred_element_type=F32)

    @pl.when(k == 0)
    def _():
        o_ref[...] = x_ref[...] + contrib

    @pl.when(k > 0)
    def _():
        o_ref[...] += contrib

    if final_norm:
        @pl.when(k == pl.num_programs(1) - 1)
        def _():
            y = o_ref[...]
            ms = jnp.mean(y * y, axis=-1, keepdims=True)
            o_ref[...] = y * lax.rsqrt(ms + RMS_EPS) * fw_ref[...]


def ffn_down(x, h, conv_w, w_down, final_w, *, final_norm, tm=512, tk=512):
    s, d = x.shape
    tm = min(tm, s)
    nk = D_FF // tk
    hb = tm // FFN_HALO
    halo_idx = lambda i: jnp.maximum(i * hb - 1, 0)
    cw = jnp.zeros((8, D_FF), F32).at[:FFN_CONV_K].set(conv_w)
    return pl.pallas_call(
        functools.partial(_ffn_down_body, final_norm=final_norm),
        grid=(s // tm, nk),
        in_specs=[pl.BlockSpec((tm, d), lambda i, k: (i, 0)),
                  pl.BlockSpec((tm, tk), lambda i, k: (i, k)),
                  pl.BlockSpec((FFN_HALO, tk), lambda i, k: (halo_idx(i), k)),
                  pl.BlockSpec((tm, tk), lambda i, k: (i, nk + k)),
                  pl.BlockSpec((8, tk), lambda i, k: (0, k)),
                  pl.BlockSpec((tk, d), lambda i, k: (k, 0)),
                  pl.BlockSpec((1, d), lambda i, k: (0, 0))],
        out_specs=pl.BlockSpec((tm, d), lambda i, k: (i, 0)),
        out_shape=jax.ShapeDtypeStruct((s, d), F32),
        scratch_shapes=[pltpu.VMEM((FFN_HALO + tm, tk), F32)],
        compiler_params=_cparams(("arbitrary", "arbitrary")),
    )(x, h, h, h, cw, w_down, final_w.reshape(1, d))


def kernel(x, positions, norm1_w, w_in, gate_b, conv_dw_w, conv_dw_b, conv_ln_w, conv_ln_b, conv_out_w,
           ssd_conv_w, ssd_conv_b, ssd_dt_bias, ssd_a_log, ssd_d, ssd_norm_w, ssd_out_w,
           da_lambda_q1, da_lambda_k1, da_lambda_q2, da_lambda_k2, da_subln_w, da_out_w,
           w_o, norm2_w, ffn_up_w, ffn_dw_w, ffn_down_w, final_norm_w):
    bsz, s_len, d = x.shape
    assert bsz == 1 and d == D_MODEL
    xc = x.reshape(s_len, d)
    pos = positions.reshape(s_len)
    for l in range(DEPTH):
        lambda_init = 0.8 - 0.6 * math.exp(-0.3 * l)
        w_main = jnp.concatenate([w_in[l][:, :DT_OFF], w_in[l][:, DT_OFF + SSD_HEADS:]], axis=1).astype(BF16)
        w_dt = jnp.zeros((d, DT_PAD), BF16).at[:, :SSD_HEADS].set(
            w_in[l][:, DT_OFF:DT_OFF + SSD_HEADS].astype(BF16))
        u, dt_raw = norm_matmul(xc, norm1_w[l], w_main, w_dt)
        h_a = conformer_pre(u, conv_dw_w[l], conv_dw_b[l], conv_ln_w[l], conv_ln_b[l])
        h_b = ssd_pre(u, dt_raw, ssd_conv_w[l], ssd_conv_b[l], ssd_dt_bias[l], ssd_a_log[l],
                      ssd_d[l], ssd_norm_w[l])
        q_r, k_r = rope_prep(u, pos)
        h_c = diff_attention(q_r, k_r, u, da_lambda_q1[l], da_lambda_k1[l], da_lambda_q2[l],
                             da_lambda_k2[l], da_subln_w[l], lambda_init)
        xc = merge_out(xc, h_a, h_b, h_c, conv_out_w[l].astype(BF16), ssd_out_w[l].astype(BF16),
                       da_out_w[l].astype(BF16), u, gate_b[l], w_o[l].astype(BF16))
        h = norm_matmul(xc, norm2_w[l], ffn_up_w[l].astype(BF16))
        xc = ffn_down(xc, h, ffn_dw_w[l], ffn_down_w[l].astype(BF16), final_norm_w,
                      final_norm=(l == DEPTH - 1))
    return xc.reshape(bsz, s_len, d)
```

```python
import functools
import math

import jax
import jax.numpy as jnp
from jax import lax
from jax.experimental import pallas as pl
from jax.experimental.pallas import tpu as pltpu

F32 = jnp.float32
BF16 = jnp.bfloat16

D_MODEL = 2048
DEPTH = 2
CONV_CH = 1024
CONV_K = 31
SSD_HEADS = 32
SSD_HEADDIM = 64
SSD_INNER = SSD_HEADS * SSD_HEADDIM
SSD_GROUPS = 4
SSD_STATE = 128
SSD_CONV_K = 4
SSD_CHUNK = 128
SSD_GROUP_W = SSD_INNER // SSD_GROUPS
DA_HEADS = 8
DA_HEAD_DIM = 64
DA_V_DIM = 128
ROPE_DIM = 16
ROPE_THETA = 500000.0
D_FF = 5632
FFN_CONV_K = 3
RMS_EPS = 1e-6
LN_EPS = 1e-5
NEG_INF = -1e30

C_CONV_A = 0
C_CONV_G = 1024
C_Z = 2048
C_X = 4096
C_B = 6144
C_C = 6656
C_Q = 7168
C_K = 8192
C_V = 9216
C_GATE = 10240
U_WIDTH = 16384
DT_OFF = 7168
DT_PAD = 128

LANES = 128
VMEM_LIMIT = 56 * 1024 * 1024


def _cparams(sem, flags=None):
    return pltpu.CompilerParams(dimension_semantics=sem, vmem_limit_bytes=VMEM_LIMIT, flags=flags)


def _norm_matmul_body(*refs, has_side, row_chunk):
    if has_side:
        x_ref, nw_ref, w_ref, ws_ref, o_ref, os_ref, xn_ref = refs
    else:
        x_ref, nw_ref, w_ref, o_ref, xn_ref = refs
    tm = x_ref.shape[0]

    @pl.when(pl.program_id(1) == 0)
    def _():
        def chunk(c, carry):
            r0 = pl.multiple_of(c * row_chunk, row_chunk)
            x = x_ref[pl.ds(r0, row_chunk), :]
            ms = jnp.mean(x * x, axis=-1, keepdims=True)
            xn_ref[pl.ds(r0, row_chunk), :] = (x * lax.rsqrt(ms + RMS_EPS) * nw_ref[...]).astype(BF16)
            return carry
        lax.fori_loop(0, tm // row_chunk, chunk, 0)
        if has_side:
            os_ref[...] = jnp.dot(xn_ref[...], ws_ref[...], preferred_element_type=F32)

    o_ref[...] = jnp.dot(xn_ref[...], w_ref[...], preferred_element_type=F32).astype(o_ref.dtype)


def norm_matmul(x, nw, w, w_side=None, *, tm=1024, tn=1024):
    s, d = x.shape
    n = w.shape[1]
    tm = min(tm, s)
    tn = min(tn, n)
    has_side = w_side is not None
    in_specs = [pl.BlockSpec((tm, d), lambda i, j: (i, 0)),
                pl.BlockSpec((1, d), lambda i, j: (0, 0)),
                pl.BlockSpec((d, tn), lambda i, j: (0, j))]
    out_shape = [jax.ShapeDtypeStruct((s, n), BF16)]
    out_specs = [pl.BlockSpec((tm, tn), lambda i, j: (i, j))]
    args = [x, nw.reshape(1, d), w]
    if has_side:
        ns = w_side.shape[1]
        in_specs.append(pl.BlockSpec((d, ns), lambda i, j: (0, 0)))
        out_shape.append(jax.ShapeDtypeStruct((s, ns), F32))
        out_specs.append(pl.BlockSpec((tm, ns), lambda i, j: (i, 0)))
        args.append(w_side)
    res = pl.pallas_call(
        functools.partial(_norm_matmul_body, has_side=has_side, row_chunk=min(128, tm)),
        grid=(s // tm, n // tn),
        in_specs=in_specs, out_specs=out_specs, out_shape=out_shape,
        scratch_shapes=[pltpu.VMEM((tm, d), BF16)],
        compiler_params=_cparams(("arbitrary", "arbitrary")), name="norm_matmul",
    )(*args)
    return res if has_side else res[0]


CONV_HALO = 32


def _conformer_body(a_ref, g_ref, ah_ref, gh_ref, w_ref, b_ref, lnw_ref, lnb_ref, o_ref, hbuf, cbuf):
    ts = a_ref.shape[0]
    i = pl.program_id(0)
    halo = ah_ref[...].astype(F32) * jax.nn.sigmoid(gh_ref[...].astype(F32))
    hbuf[0:CONV_HALO, :] = jnp.where(i > 0, halo, 0.0)
    hbuf[CONV_HALO:, :] = a_ref[...].astype(F32) * jax.nn.sigmoid(g_ref[...].astype(F32))
    base = CONV_HALO - (CONV_K - 1)
    for c in range(CONV_CH // LANES):
        cs = slice(c * LANES, (c + 1) * LANES)
        acc = jnp.zeros((ts, LANES), F32)
        for k in range(CONV_K):
            acc = acc + w_ref[k:k + 1, cs] * hbuf[base + k: base + k + ts, cs]
        cbuf[:, cs] = acc + b_ref[:, cs]
    y = cbuf[...]
    mu = jnp.mean(y, axis=-1, keepdims=True)
    yc = y - mu
    var = jnp.mean(yc * yc, axis=-1, keepdims=True)
    yn = yc * lax.rsqrt(var + LN_EPS) * lnw_ref[...] + lnb_ref[...]
    o_ref[...] = (yn * jax.nn.sigmoid(yn)).astype(o_ref.dtype)


def conformer_pre(u, w_dw, b_dw, ln_w, ln_b, *, ts=256):
    s = u.shape[0]
    ts = min(ts, s)
    hb = ts // CONV_HALO
    w_pad = jnp.zeros((32, CONV_CH), F32).at[:CONV_K].set(w_dw)
    row = lambda v: v.reshape(1, CONV_CH)
    halo_idx = lambda i: jnp.maximum(i * hb - 1, 0)
    return pl.pallas_call(
        _conformer_body,
        grid=(s // ts,),
        in_specs=[pl.BlockSpec((ts, CONV_CH), lambda i: (i, C_CONV_A // CONV_CH)),
                  pl.BlockSpec((ts, CONV_CH), lambda i: (i, C_CONV_G // CONV_CH)),
                  pl.BlockSpec((CONV_HALO, CONV_CH), lambda i: (halo_idx(i), C_CONV_A // CONV_CH)),
                  pl.BlockSpec((CONV_HALO, CONV_CH), lambda i: (halo_idx(i), C_CONV_G // CONV_CH)),
                  pl.BlockSpec((32, CONV_CH), lambda i: (0, 0)),
                  pl.BlockSpec((1, CONV_CH), lambda i: (0, 0)),
                  pl.BlockSpec((1, CONV_CH), lambda i: (0, 0)),
                  pl.BlockSpec((1, CONV_CH), lambda i: (0, 0))],
        out_specs=pl.BlockSpec((ts, CONV_CH), lambda i: (i, 0)),
        out_shape=jax.ShapeDtypeStruct((s, CONV_CH), BF16),
        scratch_shapes=[pltpu.VMEM((CONV_HALO + ts, CONV_CH), F32), pltpu.VMEM((ts, CONV_CH), F32)],
        compiler_params=_cparams(("arbitrary",)), name="conformer_pre",
    )(u, u, u, u, w_pad, row(b_dw), row(ln_w), row(ln_b))


SSD_HALO = 8


def _split3(v):
    hi = v.astype(BF16)
    r1 = v - hi.astype(F32)
    mid = r1.astype(BF16)
    lo = (r1 - mid.astype(F32)).astype(BF16)
    return hi, mid, lo


def _dot01_left(m01, v):
    hi, mid, lo = _split3(v)
    d = lambda p: jnp.dot(m01, p, preferred_element_type=F32)
    return d(hi) + d(mid) + d(lo)


def _dot01_right(v, m01):
    hi, mid, lo = _split3(v)
    d = lambda p: jnp.dot(p, m01, preferred_element_type=F32)
    return d(hi) + d(mid) + d(lo)


def _softplus(x):
    return jnp.maximum(x, 0.0) + jnp.log1p(jnp.exp(-jnp.abs(x)))


def _silu(x):
    return x * jax.nn.sigmoid(x)


def _ssd_body(x_ref, b_ref, c_ref, z_ref, xh_ref, bh_ref, ch_ref, dt_ref,
              cwx_ref, cwb_ref, cwc_ref, cbx_ref, cbb_ref, cbc_ref,
              dtb_ref, alog_ref, dx_ref, nw_ref, ex_ref,
              o_ref, state, ebuf, xs_c, b_c, c_c, ybuf):
    ts = x_ref.shape[0]
    L = SSD_CHUNK
    i = pl.program_id(0)

    @pl.when(i == 0)
    def _():
        state[...] = jnp.zeros_like(state)

    def conv(in_ref, halo_ref, w_ref, bias_ref, out_ref):
        width = in_ref.shape[1]
        ebuf[0:SSD_HALO, 0:width] = jnp.where(i > 0, halo_ref[...].astype(F32), 0.0)
        ebuf[SSD_HALO:, 0:width] = in_ref[...].astype(F32)
        base = SSD_HALO - (SSD_CONV_K - 1)
        for c in range(width // 512):
            cs = slice(c * 512, (c + 1) * 512)
            acc = jnp.zeros((ts, 512), F32)
            for k in range(SSD_CONV_K):
                acc = acc + w_ref[k:k + 1, cs] * ebuf[base + k: base + k + ts, cs]
            out_ref[:, cs] = _silu(acc + bias_ref[:, cs])

    conv(x_ref, xh_ref, cwx_ref, cbx_ref, xs_c)
    conv(b_ref, bh_ref, cwb_ref, cbb_ref, b_c)
    conv(c_ref, ch_ref, cwc_ref, cbc_ref, c_c)

    a_row = -jnp.exp(alog_ref[...])
    rr = lax.broadcasted_iota(jnp.int32, (L, L), 0)
    cc = lax.broadcasted_iota(jnp.int32, (L, L), 1)
    tri = rr >= cc
    tri01 = jnp.where(tri, 1.0, 0.0).astype(BF16)
    lane_lo = lax.broadcasted_iota(jnp.int32, (L, LANES), 1) < SSD_HEADDIM
    ex = ex_ref[...]

    def chunk(ci, carry):
        r0 = pl.multiple_of(ci * L, L)
        rows = pl.ds(r0, L)
        dt = _softplus(dt_ref[rows, :] + dtb_ref[...])
        da = dt * a_row
        cum = _dot01_left(tri01, da)
        cum_t = cum.T
        dt_t = dt.T
        cum_x = _dot01_right(cum, ex)
        dt_x = _dot01_right(dt, ex)
        cum_last = cum_x[L - 1:L, :]
        e_x = jnp.exp(cum_x)
        xs = xs_c[rows, :]
        xd = (xs * jnp.exp(cum_last - cum_x) * dt_x).astype(BF16)
        e_last = jnp.exp(cum_last)
        for g in range(SSD_GROUPS):
            gs = slice(g * SSD_GROUP_W, (g + 1) * SSD_GROUP_W)
            ns = slice(g * SSD_STATE, (g + 1) * SSD_STATE)
            bg = b_c[rows, ns]
            cg = c_c[rows, ns].astype(BF16)
            cb = lax.dot_general(cg, bg.astype(BF16), (((1,), (1,)), ((), ())),
                                 preferred_element_type=F32)
            st = state[:, gs]
            y_off = jnp.dot(cg, st.astype(BF16), preferred_element_type=F32) * e_x[:, gs]
            state[:, gs] = e_last[:, gs] * st + jnp.dot(bg.T.astype(BF16), xd[:, gs],
                                                        preferred_element_type=F32)
            for pr in range(SSD_GROUP_W // LANES):
                h0 = g * (SSD_HEADS // SSD_GROUPS) + 2 * pr
                ps = slice(h0 * SSD_HEADDIM, h0 * SSD_HEADDIM + LANES)
                x_pair = xs[:, ps].astype(BF16)
                ys = []
                for h in (h0, h0 + 1):
                    seg = cum[:, h:h + 1] - cum_t[h:h + 1, :]
                    decay = jnp.where(tri, jnp.exp(jnp.minimum(seg, 0.0)), 0.0)
                    wts = (cb * decay * dt_t[h:h + 1, :]).astype(BF16)
                    ys.append(jnp.dot(wts, x_pair, preferred_element_type=F32))
                y_pair = jnp.where(lane_lo, ys[0], ys[1])
                ybuf[:, ps] = y_pair + y_off[:, pr * LANES:(pr + 1) * LANES]
        y = ybuf[...] + xs * dx_ref[...]
        y = y * _silu(z_ref[rows, :].astype(F32))
        for g in range(SSD_GROUPS):
            gs = slice(g * SSD_GROUP_W, (g + 1) * SSD_GROUP_W)
            yg = y[:, gs]
            ms = jnp.mean(yg * yg, axis=-1, keepdims=True)
            o_ref[rows, gs] = (yg * lax.rsqrt(ms + RMS_EPS) * nw_ref[:, gs]).astype(o_ref.dtype)
        return carry

    lax.fori_loop(0, ts // L, chunk, 0)


def ssd_pre(u, dt_raw, conv_w, conv_b, dt_bias, a_log, d_skip, norm_w, *, ts=256):
    s = u.shape[0]
    ts = min(ts, s)
    hb = ts // SSD_HALO
    halo_idx = lambda i: jnp.maximum(i * hb - 1, 0)
    cw = jnp.zeros((8, conv_w.shape[1]), F32).at[:SSD_CONV_K].set(conv_w)
    n_bc = SSD_GROUPS * SSD_STATE
    cwx, cwb, cwc = cw[:, :SSD_INNER], cw[:, SSD_INNER:SSD_INNER + n_bc], cw[:, SSD_INNER + n_bc:]
    cb2 = conv_b.reshape(1, -1)
    cbx, cbb, cbc = cb2[:, :SSD_INNER], cb2[:, SSD_INNER:SSD_INNER + n_bc], cb2[:, SSD_INNER + n_bc:]
    pad_h = lambda v: jnp.zeros((1, DT_PAD), F32).at[0, :SSD_HEADS].set(v)
    dx = jnp.repeat(d_skip, SSD_HEADDIM).reshape(1, SSD_INNER)
    ex = (jnp.arange(SSD_INNER)[None, :] // SSD_HEADDIM == jnp.arange(DT_PAD)[:, None]).astype(BF16)
    const = lambda shape: pl.BlockSpec(shape, lambda i: (0, 0))
    return pl.pallas_call(
        _ssd_body,
        grid=(s // ts,),
        in_specs=[pl.BlockSpec((ts, SSD_INNER), lambda i: (i, C_X // SSD_INNER)),
                  pl.BlockSpec((ts, n_bc), lambda i: (i, C_B // n_bc)),
                  pl.BlockSpec((ts, n_bc), lambda i: (i, C_C // n_bc)),
                  pl.BlockSpec((ts, SSD_INNER), lambda i: (i, C_Z // SSD_INNER)),
                  pl.BlockSpec((SSD_HALO, SSD_INNER), lambda i: (halo_idx(i), C_X // SSD_INNER)),
                  pl.BlockSpec((SSD_HALO, n_bc), lambda i: (halo_idx(i), C_B // n_bc)),
                  pl.BlockSpec((SSD_HALO, n_bc), lambda i: (halo_idx(i), C_C // n_bc)),
                  pl.BlockSpec((ts, DT_PAD), lambda i: (i, 0)),
                  const((8, SSD_INNER)), const((8, n_bc)), const((8, n_bc)),
                  const((1, SSD_INNER)), const((1, n_bc)), const((1, n_bc)),
                  const((1, DT_PAD)), const((1, DT_PAD)), const((1, SSD_INNER)), const((1, SSD_INNER)),
                  const((DT_PAD, SSD_INNER))],
        out_specs=pl.BlockSpec((ts, SSD_INNER), lambda i: (i, 0)),
        out_shape=jax.ShapeDtypeStruct((s, SSD_INNER), BF16),
        scratch_shapes=[pltpu.VMEM((SSD_STATE, SSD_INNER), F32),
                        pltpu.VMEM((SSD_HALO + ts, SSD_INNER), F32),
                        pltpu.VMEM((ts, SSD_INNER), F32),
                        pltpu.VMEM((ts, n_bc), F32),
                        pltpu.VMEM((ts, n_bc), F32),
                        pltpu.VMEM((SSD_CHUNK, SSD_INNER), F32)],
        compiler_params=_cparams(("arbitrary",)), name="ssd_pre",
    )(u, u, u, u, u, u, u, dt_raw, cwx, cwb, cwc, cbx, cbb, cbc,
      pad_h(dt_bias), pad_h(a_log), dx, norm_w.reshape(1, SSD_INNER), ex)


LOG2E = 1.4426950408889634
ATT_TQ = 512
ATT_TK = ATT_TQ


def _rope_body(q_ref, k_ref, v_ref, pos_ref, inv_ref, qo_ref, ko_ref, vo_ref):
    ts = q_ref.shape[0]
    lane = lax.broadcasted_iota(jnp.int32, (ts, LANES), 1) % DA_HEAD_DIM
    half = ROPE_DIM // 2
    ang = pos_ref[...].astype(F32) * inv_ref[...]
    cos = jnp.cos(ang)
    sin = jnp.sin(ang)
    c_all = jnp.where(lane < ROPE_DIM, cos, 1.0)
    s_lo = jnp.where(lane < half, -sin, 0.0)
    s_hi = jnp.where((lane >= half) & (lane < ROPE_DIM), sin, 0.0)
    q_scale = DA_HEAD_DIM ** -0.5 * LOG2E

    def rotate(t):
        up = pltpu.roll(t, LANES - half, axis=1)
        dn = pltpu.roll(t, half, axis=1)
        return t * c_all + up * s_lo + dn * s_hi

    for c in range(q_ref.shape[1] // LANES):
        cs = slice(c * LANES, (c + 1) * LANES)
        ko_ref[:, cs] = rotate(k_ref[:, cs].astype(F32)).astype(ko_ref.dtype)
        qo_ref[cs, :] = (rotate(q_ref[:, cs].astype(F32)) * q_scale).T.astype(qo_ref.dtype)
        vo_ref[c] = v_ref[:, cs].astype(F32).T.astype(vo_ref.dtype)


def rope_prep(u, positions, *, ts=ATT_TK):
    s = u.shape[0]
    ts = min(ts, s)
    w = DA_HEADS * 2 * DA_HEAD_DIM
    inv = 1.0 / (ROPE_THETA ** (jnp.arange(0, ROPE_DIM, 2, dtype=F32) / ROPE_DIM))
    lane = jnp.arange(LANES) % DA_HEAD_DIM
    inv_lane = jnp.where(lane < ROPE_DIM, inv[lane % (ROPE_DIM // 2)], 0.0).reshape(1, LANES)
    return pl.pallas_call(
        _rope_body,
        grid=(s // ts,),
        in_specs=[pl.BlockSpec((ts, w), lambda i: (i, C_Q // w)),
                  pl.BlockSpec((ts, w), lambda i: (i, C_K // w)),
                  pl.BlockSpec((ts, w), lambda i: (i, C_V // w)),
                  pl.BlockSpec((ts, 1), lambda i: (i, 0)),
                  pl.BlockSpec((1, LANES), lambda i: (0, 0))],
        out_specs=[pl.BlockSpec((w, ts), lambda i: (0, i)), pl.BlockSpec((ts, w), lambda i: (i, 0)),
                   pl.BlockSpec((DA_HEADS, None, DA_V_DIM, ts), lambda i: (0, i, 0, 0))],
        out_shape=[jax.ShapeDtypeStruct((w, s), BF16), jax.ShapeDtypeStruct((s, w), BF16),
                   jax.ShapeDtypeStruct((DA_HEADS, s // ts, DA_V_DIM, ts), BF16)],
        compiler_params=_cparams(("arbitrary",)), name="rope_prep",
    )(u, u, u, positions.reshape(s, 1), inv_lane)


def _attn_body(qt_ref, k_ref, vt_ref, lq1_ref, lk1_ref, lq2_ref, lk2_ref, sw_ref, o_ref,
               m_ref, l_ref, acc_ref, sa_ref, sb_ref, ma_ref, mb_ref, *, lambda_init):
    tq = qt_ref.shape[1]
    tk = vt_ref.shape[2]
    assert tq == tk
    i = pl.program_id(1)
    row = lax.broadcasted_iota(jnp.int32, (LANES, tq), 0)
    qt = qt_ref[...]
    zero = jnp.zeros_like(qt)
    qc = (jnp.where(row < DA_HEAD_DIM, qt, zero), jnp.where(row >= DA_HEAD_DIM, qt, zero))
    m_ref[...] = jnp.full(m_ref.shape, NEG_INF, F32)
    l_ref[...] = jnp.zeros(l_ref.shape, F32)
    acc_ref[...] = jnp.zeros(acc_ref.shape, F32)

    def scores(j, s_ref, mc_ref):
        kb = k_ref[pl.ds(pl.multiple_of(j * tk, tk), tk), :]
        for c in range(2):
            s = jnp.dot(kb, qc[c], preferred_element_type=F32)
            s_ref[c] = s
            mc_ref[c] = jnp.max(s, axis=0, keepdims=True)

    def consume(j, s_ref, mc_ref, masked):
        vtb = vt_ref[j]
        for c in range(2):
            s = s_ref[c]
            if masked:
                kk = lax.broadcasted_iota(jnp.int32, (tk, tq), 0)
                qq = lax.broadcasted_iota(jnp.int32, (tk, tq), 1)
                s = jnp.where(kk <= qq, s, NEG_INF)
                m_cur = jnp.max(s, axis=0, keepdims=True)
            else:
                m_cur = mc_ref[c]
            m_prev = m_ref[c]
            m_new = jnp.maximum(m_prev, m_cur)
            alpha = jnp.exp2(m_prev - m_new)
            p = jnp.exp2(s - m_new)
            l_ref[c] = alpha * l_ref[c] + jnp.sum(p, axis=0, keepdims=True)
            acc_ref[c] = alpha * acc_ref[c] + jnp.dot(vtb, p.astype(BF16), preferred_element_type=F32)
            m_ref[c] = m_new

    def pair(t, carry):
        j0 = 2 * t
        scores(j0 + 1, sb_ref, mb_ref)
        consume(j0, sa_ref, ma_ref, False)
        scores(j0 + 2, sa_ref, ma_ref)
        consume(j0 + 1, sb_ref, mb_ref, False)
        return carry

    scores(0, sa_ref, ma_ref)
    lax.fori_loop(0, i // 2, pair, 0)

    @pl.when(i % 2 == 0)
    def _():
        consume(i, sa_ref, ma_ref, True)

    @pl.when(i % 2 == 1)
    def _():
        scores(i, sb_ref, mb_ref)
        consume(i - 1, sa_ref, ma_ref, False)
        consume(i, sb_ref, mb_ref, True)

    lam = (jnp.exp(jnp.sum(lq1_ref[...] * lk1_ref[...], axis=-1, keepdims=True))
           - jnp.exp(jnp.sum(lq2_ref[...] * lk2_ref[...], axis=-1, keepdims=True)) + lambda_init)
    ot = acc_ref[0] / l_ref[0] - lam * (acc_ref[1] / l_ref[1])
    ms = jnp.mean(ot * ot, axis=0, keepdims=True)
    ot = ot * (lax.rsqrt(ms + LN_EPS) * (1.0 - lambda_init))
    o_ref[...] = (ot.T * sw_ref[...]).astype(o_ref.dtype)


def diff_attention(q_t, k_r, v_t, lq1, lk1, lq2, lk2, subln_w, lambda_init, *, tq=ATT_TQ):
    s = k_r.shape[0]
    nkb, tk = v_t.shape[1], v_t.shape[3]
    tq = min(tq, s)
    row = lambda v: v.reshape(1, -1)
    const = lambda n: pl.BlockSpec((1, n), lambda h, i: (0, 0))
    return pl.pallas_call(
        functools.partial(_attn_body, lambda_init=lambda_init),
        grid=(DA_HEADS, s // tq),
        in_specs=[pl.BlockSpec((LANES, tq), lambda h, i: (h, i)),
                  pl.BlockSpec((s, LANES), lambda h, i: (0, h)),
                  pl.BlockSpec((None, nkb, DA_V_DIM, tk), lambda h, i: (h, 0, 0, 0)),
                  const(DA_HEAD_DIM), const(DA_HEAD_DIM), const(DA_HEAD_DIM), const(DA_HEAD_DIM),
                  const(DA_V_DIM)],
        out_specs=pl.BlockSpec((tq, LANES), lambda h, i: (i, h)),
        out_shape=jax.ShapeDtypeStruct((s, DA_HEADS * DA_V_DIM), BF16),
        scratch_shapes=[pltpu.VMEM((2, 1, tq), F32), pltpu.VMEM((2, 1, tq), F32),
                        pltpu.VMEM((2, DA_V_DIM, tq), F32),
                        pltpu.VMEM((2, tk, tq), F32), pltpu.VMEM((2, tk, tq), F32),
                        pltpu.VMEM((2, 1, tq), F32), pltpu.VMEM((2, 1, tq), F32)],
        compiler_params=_cparams(("arbitrary", "arbitrary")), name="diff_attention",
    )(q_t, k_r, v_t, row(lq1), row(lk1), row(lq2), row(lk2), row(subln_w))


def _merge_body(ha_ref, hb_ref, hc_ref, wa_ref, wb_ref, wc_ref, ga_ref, gb_ref, gc_ref, bias_ref, o_ref):
    def gated(h_ref, w_ref, g_ref, bi):
        y = jnp.dot(h_ref[...], w_ref[...], preferred_element_type=F32)
        return jax.nn.sigmoid(g_ref[...].astype(F32) + bias_ref[bi:bi + 1, :]) * y

    m = gated(ha_ref, wa_ref, ga_ref, 0) + gated(hb_ref, wb_ref, gb_ref, 1) + gated(hc_ref, wc_ref, gc_ref, 2)
    o_ref[...] = m.astype(o_ref.dtype)


def gated_merge(h_a, h_b, h_c, w_a, w_b, w_c, u, gate_b, *, tm=1024, tn=512):
    s = h_a.shape[0]
    d = w_a.shape[1]
    tm = min(tm, s)
    gblk = C_GATE // tn
    nblk = d // tn
    return pl.pallas_call(
        _merge_body,
        grid=(s // tm, d // tn),
        in_specs=[pl.BlockSpec((tm, h_a.shape[1]), lambda i, j: (i, 0)),
                  pl.BlockSpec((tm, h_b.shape[1]), lambda i, j: (i, 0)),
                  pl.BlockSpec((tm, h_c.shape[1]), lambda i, j: (i, 0)),
                  pl.BlockSpec((w_a.shape[0], tn), lambda i, j: (0, j)),
                  pl.BlockSpec((w_b.shape[0], tn), lambda i, j: (0, j)),
                  pl.BlockSpec((w_c.shape[0], tn), lambda i, j: (0, j)),
                  pl.BlockSpec((tm, tn), lambda i, j: (i, gblk + j)),
                  pl.BlockSpec((tm, tn), lambda i, j: (i, gblk + nblk + j)),
                  pl.BlockSpec((tm, tn), lambda i, j: (i, gblk + 2 * nblk + j)),
                  pl.BlockSpec((3, tn), lambda i, j: (0, j))],
        out_specs=pl.BlockSpec((tm, tn), lambda i, j: (i, j)),
        out_shape=jax.ShapeDtypeStruct((s, d), BF16),
        compiler_params=_cparams(("arbitrary", "arbitrary")), name="gated_merge",
    )(h_a, h_b, h_c, w_a, w_b, w_c, u, u, u, gate_b)


def _residual_matmul_body(x_ref, a_ref, w_ref, o_ref):
    o_ref[...] = x_ref[...] + jnp.dot(a_ref[...], w_ref[...], preferred_element_type=F32)


def residual_matmul(x, a, w, *, tm=512):
    s, d = x.shape
    tm = min(tm, s)
    return pl.pallas_call(
        _residual_matmul_body,
        grid=(s // tm,),
        in_specs=[pl.BlockSpec((tm, d), lambda i: (i, 0)),
                  pl.BlockSpec((tm, a.shape[1]), lambda i: (i, 0)),
                  pl.BlockSpec(w.shape, lambda i: (0, 0))],
        out_specs=pl.BlockSpec((tm, d), lambda i: (i, 0)),
        out_shape=jax.ShapeDtypeStruct((s, d), F32),
        compiler_params=_cparams(("arbitrary",)), name="residual_matmul",
    )(x, a, w)


FFN_HALO = 8


def _ffn_down_body(x_ref, g_ref, gh_ref, up_ref, cw_ref, wd_ref, fw_ref, o_ref, ebuf, *, final_norm):
    tm = g_ref.shape[0]
    i = pl.program_id(0)
    k = pl.program_id(1)
    ebuf[0:FFN_HALO, :] = jnp.where(i > 0, gh_ref[...].astype(F32), 0.0)
    ebuf[FFN_HALO:, :] = g_ref[...].astype(F32)
    base = FFN_HALO - (FFN_CONV_K - 1)
    acc = jnp.zeros(g_ref.shape, F32)
    for t in range(FFN_CONV_K):
        acc = acc + cw_ref[t:t + 1, :] * ebuf[base + t: base + t + tm, :]
    act = (_silu(acc) * up_ref[...].astype(F32)).astype(BF16)
    contrib = jnp.dot(act, wd_ref[...], preferred_element_type=F32)

    @pl.when(k == 0)
    def _():
        o_ref[...] = x_ref[...] + contrib

    @pl.when(k > 0)
    def _():
        o_ref[...] += contrib

    if final_norm:
        @pl.when(k == pl.num_programs(1) - 1)
        def _():
            y = o_ref[...]
            ms = jnp.mean(y * y, axis=-1, keepdims=True)
            o_ref[...] = y * lax.rsqrt(ms + RMS_EPS) * fw_ref[...]


def ffn_down(x, h, conv_w, w_down, final_w, *, final_norm, tm=512, tk=1408):
    s, d = x.shape
    tm = min(tm, s)
    nk = D_FF // tk
    hb = tm // FFN_HALO
    halo_idx = lambda i: jnp.maximum(i * hb - 1, 0)
    cw = jnp.zeros((8, D_FF), F32).at[:FFN_CONV_K].set(conv_w)
    return pl.pallas_call(
        functools.partial(_ffn_down_body, final_norm=final_norm),
        grid=(s // tm, nk),
        in_specs=[pl.BlockSpec((tm, d), lambda i, k: (i, 0)),
                  pl.BlockSpec((tm, tk), lambda i, k: (i, k)),
                  pl.BlockSpec((FFN_HALO, tk), lambda i, k: (halo_idx(i), k)),
                  pl.BlockSpec((tm, tk), lambda i, k: (i, nk + k)),
                  pl.BlockSpec((8, tk), lambda i, k: (0, k)),
                  pl.BlockSpec((tk, d), lambda i, k: (k, 0)),
                  pl.BlockSpec((1, d), lambda i, k: (0, 0))],
        out_specs=pl.BlockSpec((tm, d), lambda i, k: (i, 0)),
        out_shape=jax.ShapeDtypeStruct((s, d), F32),
        scratch_shapes=[pltpu.VMEM((FFN_HALO + tm, tk), F32)],
        compiler_params=_cparams(("arbitrary", "arbitrary")), name="ffn_down",
    )(x, h, h, h, cw, w_down, final_w.reshape(1, d))


def kernel(x, positions, norm1_w, w_in, gate_b, conv_dw_w, conv_dw_b, conv_ln_w, conv_ln_b, conv_out_w,
           ssd_conv_w, ssd_conv_b, ssd_dt_bias, ssd_a_log, ssd_d, ssd_norm_w, ssd_out_w,
           da_lambda_q1, da_lambda_k1, da_lambda_q2, da_lambda_k2, da_subln_w, da_out_w,
           w_o, norm2_w, ffn_up_w, ffn_dw_w, ffn_down_w, final_norm_w):
    bsz, s_len, d = x.shape
    assert bsz == 1 and d == D_MODEL
    xc = x.reshape(s_len, d)
    pos = positions.reshape(s_len)
    for l in range(DEPTH):
        lambda_init = 0.8 - 0.6 * math.exp(-0.3 * l)
        w_main = jnp.concatenate([w_in[l][:, :DT_OFF], w_in[l][:, DT_OFF + SSD_HEADS:]], axis=1).astype(BF16)
        w_dt = jnp.zeros((d, DT_PAD), BF16).at[:, :SSD_HEADS].set(
            w_in[l][:, DT_OFF:DT_OFF + SSD_HEADS].astype(BF16))
        u, dt_raw = norm_matmul(xc, norm1_w[l], w_main, w_dt)
        h_a = conformer_pre(u, conv_dw_w[l], conv_dw_b[l], conv_ln_w[l], conv_ln_b[l])
        h_b = ssd_pre(u, dt_raw, ssd_conv_w[l], ssd_conv_b[l], ssd_dt_bias[l], ssd_a_log[l],
                      ssd_d[l], ssd_norm_w[l])
        q_t, k_r, v_t = rope_prep(u, pos)
        h_c = diff_attention(q_t, k_r, v_t, da_lambda_q1[l], da_lambda_k1[l], da_lambda_q2[l],
                             da_lambda_k2[l], da_subln_w[l], lambda_init)
        merged = gated_merge(h_a, h_b, h_c, conv_out_w[l].astype(BF16), ssd_out_w[l].astype(BF16),
                             da_out_w[l].astype(BF16), u, gate_b[l])
        xc = residual_matmul(xc, merged, w_o[l].astype(BF16))
        h = norm_matmul(xc, norm2_w[l], ffn_up_w[l].astype(BF16))
        xc = ffn_down(xc, h, ffn_dw_w[l], ffn_down_w[l].astype(BF16), final_norm_w,
                      final_norm=(l == DEPTH - 1))
    return xc.reshape(bsz, s_len, d)
```

```python
import functools
import math

import jax
import jax.numpy as jnp
from jax import lax
from jax.experimental import pallas as pl
from jax.experimental.pallas import tpu as pltpu

F32 = jnp.float32
BF16 = jnp.bfloat16

D_MODEL = 2048
DEPTH = 2
CONV_CH = 1024
CONV_K = 31
SSD_HEADS = 32
SSD_HEADDIM = 64
SSD_INNER = SSD_HEADS * SSD_HEADDIM
SSD_GROUPS = 4
SSD_STATE = 128
SSD_CONV_K = 4
SSD_CHUNK = 128
SSD_GROUP_W = SSD_INNER // SSD_GROUPS
DA_HEADS = 8
DA_HEAD_DIM = 64
DA_V_DIM = 128
ROPE_DIM = 16
ROPE_THETA = 500000.0
D_FF = 5632
FFN_CONV_K = 3
RMS_EPS = 1e-6
LN_EPS = 1e-5
NEG_INF = -1e30

C_CONV_A = 0
C_CONV_G = 1024
C_Z = 2048
C_X = 4096
C_B = 6144
C_C = 6656
C_Q = 7168
C_K = 8192
C_V = 9216
C_GATE = 10240
U_WIDTH = 16384
DT_OFF = 7168
DT_PAD = 128

LANES = 128
VMEM_LIMIT = 56 * 1024 * 1024


def _cparams(sem, flags=None):
    return pltpu.CompilerParams(dimension_semantics=sem, vmem_limit_bytes=VMEM_LIMIT, flags=flags)


def _norm_matmul_body(*refs, has_side, row_chunk):
    if has_side:
        x_ref, nw_ref, w_ref, ws_ref, o_ref, os_ref, xn_ref = refs
    else:
        x_ref, nw_ref, w_ref, o_ref, xn_ref = refs
    tm = x_ref.shape[0]

    @pl.when(pl.program_id(1) == 0)
    def _():
        def chunk(c, carry):
            r0 = pl.multiple_of(c * row_chunk, row_chunk)
            x = x_ref[pl.ds(r0, row_chunk), :]
            ms = jnp.mean(x * x, axis=-1, keepdims=True)
            xn_ref[pl.ds(r0, row_chunk), :] = (x * lax.rsqrt(ms + RMS_EPS) * nw_ref[...]).astype(BF16)
            return carry
        lax.fori_loop(0, tm // row_chunk, chunk, 0)
        if has_side:
            os_ref[...] = jnp.dot(xn_ref[...], ws_ref[...], preferred_element_type=F32)

    o_ref[...] = jnp.dot(xn_ref[...], w_ref[...], preferred_element_type=F32).astype(o_ref.dtype)


def norm_matmul(x, nw, w, w_side=None, *, tm=1024, tn=1024):
    s, d = x.shape
    n = w.shape[1]
    tm = min(tm, s)
    tn = min(tn, n)
    has_side = w_side is not None
    in_specs = [pl.BlockSpec((tm, d), lambda i, j: (i, 0)),
                pl.BlockSpec((1, d), lambda i, j: (0, 0)),
                pl.BlockSpec((d, tn), lambda i, j: (0, j))]
    out_shape = [jax.ShapeDtypeStruct((s, n), BF16)]
    out_specs = [pl.BlockSpec((tm, tn), lambda i, j: (i, j))]
    args = [x, nw.reshape(1, d), w]
    if has_side:
        ns = w_side.shape[1]
        in_specs.append(pl.BlockSpec((d, ns), lambda i, j: (0, 0)))
        out_shape.append(jax.ShapeDtypeStruct((s, ns), F32))
        out_specs.append(pl.BlockSpec((tm, ns), lambda i, j: (i, 0)))
        args.append(w_side)
    res = pl.pallas_call(
        functools.partial(_norm_matmul_body, has_side=has_side, row_chunk=min(128, tm)),
        grid=(s // tm, n // tn),
        in_specs=in_specs, out_specs=out_specs, out_shape=out_shape,
        scratch_shapes=[pltpu.VMEM((tm, d), BF16)],
        compiler_params=_cparams(("arbitrary", "arbitrary")), name="norm_matmul",
    )(*args)
    return res if has_side else res[0]


CONV_HALO = 32
CONV_ROWS = 128


def _conformer_body(a_ref, g_ref, ah_ref, gh_ref, w_ref, b_ref, lnw_ref, lnb_ref, o_ref, hbuf, cbuf):
    ts = a_ref.shape[0]
    i = pl.program_id(0)
    halo = ah_ref[...].astype(F32) * jax.nn.sigmoid(gh_ref[...].astype(F32))
    hbuf[0:CONV_HALO, :] = jnp.where(i > 0, halo, 0.0)
    hbuf[CONV_HALO:, :] = a_ref[...].astype(F32) * jax.nn.sigmoid(g_ref[...].astype(F32))
    base = CONV_HALO - (CONV_K - 1)
    sub = 8
    rows = CONV_ROWS
    ext = rows + CONV_HALO
    for c in range(CONV_CH // LANES):
        cs = slice(c * LANES, (c + 1) * LANES)
        for r0 in range(0, ts, rows):
            slab = hbuf[r0:r0 + ext, cs]
            acc = jnp.zeros((rows, LANES), F32)
            for res in range(sub):
                taps = [k for k in range(CONV_K) if (base + k) % sub == res]
                if not taps:
                    continue
                rolled = slab if res == 0 else pltpu.roll(slab, ext - res, axis=0)
                for k in taps:
                    a = (base + k) // sub * sub
                    acc = acc + w_ref[k:k + 1, cs] * rolled[a:a + rows]
            cbuf[r0:r0 + rows, cs] = acc + b_ref[:, cs]
    y = cbuf[...]
    mu = jnp.mean(y, axis=-1, keepdims=True)
    yc = y - mu
    var = jnp.mean(yc * yc, axis=-1, keepdims=True)
    yn = yc * lax.rsqrt(var + LN_EPS) * lnw_ref[...] + lnb_ref[...]
    o_ref[...] = (yn * jax.nn.sigmoid(yn)).astype(o_ref.dtype)


def conformer_pre(u, w_dw, b_dw, ln_w, ln_b, *, ts=256):
    s = u.shape[0]
    ts = min(ts, s)
    hb = ts // CONV_HALO
    w_pad = jnp.zeros((32, CONV_CH), F32).at[:CONV_K].set(w_dw)
    row = lambda v: v.reshape(1, CONV_CH)
    halo_idx = lambda i: jnp.maximum(i * hb - 1, 0)
    return pl.pallas_call(
        _conformer_body,
        grid=(s // ts,),
        in_specs=[pl.BlockSpec((ts, CONV_CH), lambda i: (i, C_CONV_A // CONV_CH)),
                  pl.BlockSpec((ts, CONV_CH), lambda i: (i, C_CONV_G // CONV_CH)),
                  pl.BlockSpec((CONV_HALO, CONV_CH), lambda i: (halo_idx(i), C_CONV_A // CONV_CH)),
                  pl.BlockSpec((CONV_HALO, CONV_CH), lambda i: (halo_idx(i), C_CONV_G // CONV_CH)),
                  pl.BlockSpec((32, CONV_CH), lambda i: (0, 0)),
                  pl.BlockSpec((1, CONV_CH), lambda i: (0, 0)),
                  pl.BlockSpec((1, CONV_CH), lambda i: (0, 0)),
                  pl.BlockSpec((1, CONV_CH), lambda i: (0, 0))],
        out_specs=pl.BlockSpec((ts, CONV_CH), lambda i: (i, 0)),
        out_shape=jax.ShapeDtypeStruct((s, CONV_CH), BF16),
        scratch_shapes=[pltpu.VMEM((CONV_HALO + ts, CONV_CH), F32), pltpu.VMEM((ts, CONV_CH), F32)],
        compiler_params=_cparams(("arbitrary",)), name="conformer_pre",
    )(u, u, u, u, w_pad, row(b_dw), row(ln_w), row(ln_b))


SSD_HALO = 8
SSD_CONV_ROWS = 64
SSD_CONV_LANES = 256


def _split3(v):
    hi = v.astype(BF16)
    r1 = v - hi.astype(F32)
    mid = r1.astype(BF16)
    lo = (r1 - mid.astype(F32)).astype(BF16)
    return hi, mid, lo


def _dot01_left(m01, v):
    hi, mid, lo = _split3(v)
    d = lambda p: jnp.dot(m01, p, preferred_element_type=F32)
    return d(hi) + d(mid) + d(lo)


def _softplus(x):
    return jnp.maximum(x, 0.0) + jnp.log1p(jnp.exp(-jnp.abs(x)))


def _silu(x):
    return x * jax.nn.sigmoid(x)


def _ssd_body(x_ref, b_ref, c_ref, z_ref, xh_ref, bh_ref, ch_ref, dt_ref,
              cwx_ref, cwb_ref, cwc_ref, cbx_ref, cbb_ref, cbc_ref,
              dtb_ref, alog_ref, dx_ref, nw_ref, ex_ref,
              o_ref, state, ebuf, xs_c, b_c, c_c, ybuf, dt_s, cum_s, cx_s, dtx_s):
    ts = x_ref.shape[0]
    L = SSD_CHUNK
    i = pl.program_id(0)

    @pl.when(i == 0)
    def _():
        state[...] = jnp.zeros_like(state)

    def conv(in_ref, halo_ref, w_ref, bias_ref, out_ref):
        width = in_ref.shape[1]
        ebuf[0:SSD_HALO, 0:width] = jnp.where(i > 0, halo_ref[...].astype(F32), 0.0)
        ebuf[SSD_HALO:, 0:width] = in_ref[...].astype(F32)
        base = SSD_HALO - (SSD_CONV_K - 1)
        rb, cw = SSD_CONV_ROWS, SSD_CONV_LANES
        ext = rb + SSD_HALO

        def row_block(bi, carry):
            r0 = pl.multiple_of(bi * rb, rb)
            for c in range(width // cw):
                cs = slice(c * cw, (c + 1) * cw)
                slab = ebuf[pl.ds(r0, ext), cs]
                acc = bias_ref[:, cs]
                for k in range(SSD_CONV_K):
                    a, res = (base + k) // 8 * 8, (base + k) % 8
                    rolled = slab if res == 0 else pltpu.roll(slab, ext - res, axis=0)
                    acc = acc + w_ref[k:k + 1, cs] * rolled[a:a + rb]
                out_ref[pl.ds(r0, rb), cs] = _silu(acc)
            return carry

        lax.fori_loop(0, ts // rb, row_block, 0)

    conv(x_ref, xh_ref, cwx_ref, cbx_ref, xs_c)
    conv(b_ref, bh_ref, cwb_ref, cbb_ref, b_c)
    conv(c_ref, ch_ref, cwc_ref, cbc_ref, c_c)

    a_row = -jnp.exp(alog_ref[...])
    rr = lax.broadcasted_iota(jnp.int32, (L, L), 0)
    cc = lax.broadcasted_iota(jnp.int32, (L, L), 1)
    tri = rr >= cc
    tri01 = jnp.where(tri, 1.0, 0.0).astype(BF16)
    lane_lo = lax.broadcasted_iota(jnp.int32, (L, LANES), 1) < SSD_HEADDIM

    dt_all = _softplus(dt_ref[...] + dtb_ref[...])
    dt_s[...] = dt_all
    for ci in range(ts // L):
        cum_s[ci * L:(ci + 1) * L, :] = _dot01_left(tri01, dt_all[ci * L:(ci + 1) * L] * a_row)
    for src, dst in ((cum_s, cx_s), (dt_s, dtx_s)):
        dst[...] = jnp.dot(jnp.concatenate(_split3(src[...]), axis=1), ex_ref[...],
                           preferred_element_type=F32)

    def chunk(ci, carry):
        r0 = pl.multiple_of(ci * L, L)
        rows = pl.ds(r0, L)
        dt = dt_s[rows, :]
        cum = cum_s[rows, :]
        cum_t = cum.T
        dt_t = dt.T
        cum_x = cx_s[rows, :]
        dt_x = dtx_s[rows, :]
        cum_last = cum_x[L - 1:L, :]
        e_x = jnp.exp(cum_x)
        xs = xs_c[rows, :]
        xd = (xs * jnp.exp(cum_last - cum_x) * dt_x).astype(BF16)
        e_last = jnp.exp(cum_last)
        for g in range(SSD_GROUPS):
            gs = slice(g * SSD_GROUP_W, (g + 1) * SSD_GROUP_W)
            ns = slice(g * SSD_STATE, (g + 1) * SSD_STATE)
            bg = b_c[rows, ns]
            cg = c_c[rows, ns].astype(BF16)
            cb = lax.dot_general(cg, bg.astype(BF16), (((1,), (1,)), ((), ())),
                                 preferred_element_type=F32)
            st = state[:, gs]
            y_off = jnp.dot(cg, st.astype(BF16), preferred_element_type=F32) * e_x[:, gs]
            state[:, gs] = e_last[:, gs] * st + jnp.dot(bg.T.astype(BF16), xd[:, gs],
                                                        preferred_element_type=F32)
            for pr in range(SSD_GROUP_W // LANES):
                h0 = g * (SSD_HEADS // SSD_GROUPS) + 2 * pr
                ps = slice(h0 * SSD_HEADDIM, h0 * SSD_HEADDIM + LANES)
                x_pair = xs[:, ps].astype(BF16)
                ys = []
                for h in (h0, h0 + 1):
                    seg = cum[:, h:h + 1] - cum_t[h:h + 1, :]
                    decay = jnp.where(tri, jnp.exp(jnp.minimum(seg, 0.0)), 0.0)
                    wts = (cb * decay * dt_t[h:h + 1, :]).astype(BF16)
                    ys.append(jnp.dot(wts, x_pair, preferred_element_type=F32))
                y_pair = jnp.where(lane_lo, ys[0], ys[1])
                ybuf[:, ps] = y_pair + y_off[:, pr * LANES:(pr + 1) * LANES]
        y = ybuf[...] + xs * dx_ref[...]
        y = y * _silu(z_ref[rows, :].astype(F32))
        for g in range(SSD_GROUPS):
            gs = slice(g * SSD_GROUP_W, (g + 1) * SSD_GROUP_W)
            yg = y[:, gs]
            ms = jnp.mean(yg * yg, axis=-1, keepdims=True)
            o_ref[rows, gs] = (yg * lax.rsqrt(ms + RMS_EPS) * nw_ref[:, gs]).astype(o_ref.dtype)
        return carry

    lax.fori_loop(0, ts // L, chunk, 0)


def ssd_pre(u, dt_raw, conv_w, conv_b, dt_bias, a_log, d_skip, norm_w, *, ts=512):
    s = u.shape[0]
    ts = min(ts, s)
    hb = ts // SSD_HALO
    halo_idx = lambda i: jnp.maximum(i * hb - 1, 0)
    cw = jnp.zeros((8, conv_w.shape[1]), F32).at[:SSD_CONV_K].set(conv_w)
    n_bc = SSD_GROUPS * SSD_STATE
    cwx, cwb, cwc = cw[:, :SSD_INNER], cw[:, SSD_INNER:SSD_INNER + n_bc], cw[:, SSD_INNER + n_bc:]
    cb2 = conv_b.reshape(1, -1)
    cbx, cbb, cbc = cb2[:, :SSD_INNER], cb2[:, SSD_INNER:SSD_INNER + n_bc], cb2[:, SSD_INNER + n_bc:]
    pad_h = lambda v: jnp.zeros((1, DT_PAD), F32).at[0, :SSD_HEADS].set(v)
    dx = jnp.repeat(d_skip, SSD_HEADDIM).reshape(1, SSD_INNER)
    ex = (jnp.arange(SSD_INNER)[None, :] // SSD_HEADDIM == jnp.arange(DT_PAD)[:, None]).astype(BF16)
    ex = jnp.concatenate([ex, ex, ex], axis=0)
    const = lambda shape: pl.BlockSpec(shape, lambda i: (0, 0))
    return pl.pallas_call(
        _ssd_body,
        grid=(s // ts,),
        in_specs=[pl.BlockSpec((ts, SSD_INNER), lambda i: (i, C_X // SSD_INNER)),
                  pl.BlockSpec((ts, n_bc), lambda i: (i, C_B // n_bc)),
                  pl.BlockSpec((ts, n_bc), lambda i: (i, C_C // n_bc)),
                  pl.BlockSpec((ts, SSD_INNER), lambda i: (i, C_Z // SSD_INNER)),
                  pl.BlockSpec((SSD_HALO, SSD_INNER), lambda i: (halo_idx(i), C_X // SSD_INNER)),
                  pl.BlockSpec((SSD_HALO, n_bc), lambda i: (halo_idx(i), C_B // n_bc)),
                  pl.BlockSpec((SSD_HALO, n_bc), lambda i: (halo_idx(i), C_C // n_bc)),
                  pl.BlockSpec((ts, DT_PAD), lambda i: (i, 0)),
                  const((8, SSD_INNER)), const((8, n_bc)), const((8, n_bc)),
                  const((1, SSD_INNER)), const((1, n_bc)), const((1, n_bc)),
                  const((1, DT_PAD)), const((1, DT_PAD)), const((1, SSD_INNER)), const((1, SSD_INNER)),
                  const((3 * DT_PAD, SSD_INNER))],
        out_specs=pl.BlockSpec((ts, SSD_INNER), lambda i: (i, 0)),
        out_shape=jax.ShapeDtypeStruct((s, SSD_INNER), BF16),
        scratch_shapes=[pltpu.VMEM((SSD_STATE, SSD_INNER), F32),
                        pltpu.VMEM((SSD_HALO + ts, SSD_INNER), F32),
                        pltpu.VMEM((ts, SSD_INNER), F32),
                        pltpu.VMEM((ts, n_bc), F32),
                        pltpu.VMEM((ts, n_bc), F32),
                        pltpu.VMEM((SSD_CHUNK, SSD_INNER), F32),
                        pltpu.VMEM((ts, DT_PAD), F32), pltpu.VMEM((ts, DT_PAD), F32),
                        pltpu.VMEM((ts, SSD_INNER), F32), pltpu.VMEM((ts, SSD_INNER), F32)],
        compiler_params=_cparams(("arbitrary",)), name="ssd_pre",
    )(u, u, u, u, u, u, u, dt_raw, cwx, cwb, cwc, cbx, cbb, cbc,
      pad_h(dt_bias), pad_h(a_log), dx, norm_w.reshape(1, SSD_INNER), ex)


LOG2E = 1.4426950408889634
ATT_TQ = 512
ATT_TK = ATT_TQ


def _rope_body(q_ref, k_ref, v_ref, pos_ref, inv_ref, qo_ref, ko_ref, vo_ref):
    ts = q_ref.shape[0]
    lane = lax.broadcasted_iota(jnp.int32, (ts, LANES), 1) % DA_HEAD_DIM
    half = ROPE_DIM // 2
    ang = pos_ref[...].astype(F32) * inv_ref[...]
    cos = jnp.cos(ang)
    sin = jnp.sin(ang)
    c_all = jnp.where(lane < ROPE_DIM, cos, 1.0)
    s_lo = jnp.where(lane < half, -sin, 0.0)
    s_hi = jnp.where((lane >= half) & (lane < ROPE_DIM), sin, 0.0)
    q_scale = DA_HEAD_DIM ** -0.5 * LOG2E

    def rotate(t):
        up = pltpu.roll(t, LANES - half, axis=1)
        dn = pltpu.roll(t, half, axis=1)
        return t * c_all + up * s_lo + dn * s_hi

    for c in range(q_ref.shape[1] // LANES):
        cs = slice(c * LANES, (c + 1) * LANES)
        ko_ref[:, cs] = rotate(k_ref[:, cs].astype(F32)).astype(ko_ref.dtype)
        qo_ref[cs, :] = (rotate(q_ref[:, cs].astype(F32)) * q_scale).T.astype(qo_ref.dtype)
        vo_ref[c] = v_ref[:, cs].astype(F32).T.astype(vo_ref.dtype)


def rope_prep(u, positions, *, ts=ATT_TK):
    s = u.shape[0]
    ts = min(ts, s)
    w = DA_HEADS * 2 * DA_HEAD_DIM
    inv = 1.0 / (ROPE_THETA ** (jnp.arange(0, ROPE_DIM, 2, dtype=F32) / ROPE_DIM))
    lane = jnp.arange(LANES) % DA_HEAD_DIM
    inv_lane = jnp.where(lane < ROPE_DIM, inv[lane % (ROPE_DIM // 2)], 0.0).reshape(1, LANES)
    return pl.pallas_call(
        _rope_body,
        grid=(s // ts,),
        in_specs=[pl.BlockSpec((ts, w), lambda i: (i, C_Q // w)),
                  pl.BlockSpec((ts, w), lambda i: (i, C_K // w)),
                  pl.BlockSpec((ts, w), lambda i: (i, C_V // w)),
                  pl.BlockSpec((ts, 1), lambda i: (i, 0)),
                  pl.BlockSpec((1, LANES), lambda i: (0, 0))],
        out_specs=[pl.BlockSpec((w, ts), lambda i: (0, i)), pl.BlockSpec((ts, w), lambda i: (i, 0)),
                   pl.BlockSpec((DA_HEADS, None, DA_V_DIM, ts), lambda i: (0, i, 0, 0))],
        out_shape=[jax.ShapeDtypeStruct((w, s), BF16), jax.ShapeDtypeStruct((s, w), BF16),
                   jax.ShapeDtypeStruct((DA_HEADS, s // ts, DA_V_DIM, ts), BF16)],
        compiler_params=_cparams(("arbitrary",)), name="rope_prep",
    )(u, u, u, positions.reshape(s, 1), inv_lane)


def _attn_body(qt_ref, k_ref, vt_ref, lq1_ref, lk1_ref, lq2_ref, lk2_ref, sw_ref, o_ref,
               m_ref, l_ref, acc_ref, sa_ref, sb_ref, ma_ref, mb_ref, *, lambda_init):
    tq = qt_ref.shape[1]
    tk = vt_ref.shape[2]
    assert tq == tk
    i = pl.program_id(1)
    row = lax.broadcasted_iota(jnp.int32, (LANES, tq), 0)
    qt = qt_ref[...]
    zero = jnp.zeros_like(qt)
    qc = (jnp.where(row < DA_HEAD_DIM, qt, zero), jnp.where(row >= DA_HEAD_DIM, qt, zero))
    m_ref[...] = jnp.full(m_ref.shape, NEG_INF, F32)
    l_ref[...] = jnp.zeros(l_ref.shape, F32)
    acc_ref[...] = jnp.zeros(acc_ref.shape, F32)

    def scores(j, s_ref, mc_ref):
        kb = k_ref[pl.ds(pl.multiple_of(j * tk, tk), tk), :]
        for c in range(2):
            s = jnp.dot(kb, qc[c], preferred_element_type=F32)
            s_ref[c] = s
            mc_ref[c] = jnp.max(s, axis=0, keepdims=True)

    def consume(j, s_ref, mc_ref, masked):
        vtb = vt_ref[j]
        for c in range(2):
            s = s_ref[c]
            if masked:
                kk = lax.broadcasted_iota(jnp.int32, (tk, tq), 0)
                qq = lax.broadcasted_iota(jnp.int32, (tk, tq), 1)
                s = jnp.where(kk <= qq, s, NEG_INF)
                m_cur = jnp.max(s, axis=0, keepdims=True)
            else:
                m_cur = mc_ref[c]
            m_prev = m_ref[c]
            m_new = jnp.maximum(m_prev, m_cur)
            alpha = jnp.exp2(m_prev - m_new)
            p = jnp.exp2(s - m_new)
            l_ref[c] = alpha * l_ref[c] + jnp.sum(p, axis=0, keepdims=True)
            acc_ref[c] = alpha * acc_ref[c] + jnp.dot(vtb, p.astype(BF16), preferred_element_type=F32)
            m_ref[c] = m_new

    def pair(t, carry):
        j0 = 2 * t
        scores(j0 + 1, sb_ref, mb_ref)
        consume(j0, sa_ref, ma_ref, False)
        scores(j0 + 2, sa_ref, ma_ref)
        consume(j0 + 1, sb_ref, mb_ref, False)
        return carry

    scores(0, sa_ref, ma_ref)
    lax.fori_loop(0, i // 2, pair, 0)

    @pl.when(i % 2 == 0)
    def _():
        consume(i, sa_ref, ma_ref, True)

    @pl.when(i % 2 == 1)
    def _():
        scores(i, sb_ref, mb_ref)
        consume(i - 1, sa_ref, ma_ref, False)
        consume(i, sb_ref, mb_ref, True)

    lam = (jnp.exp(jnp.sum(lq1_ref[...] * lk1_ref[...], axis=-1, keepdims=True))
           - jnp.exp(jnp.sum(lq2_ref[...] * lk2_ref[...], axis=-1, keepdims=True)) + lambda_init)
    ot = acc_ref[0] / l_ref[0] - lam * (acc_ref[1] / l_ref[1])
    ms = jnp.mean(ot * ot, axis=0, keepdims=True)
    ot = ot * (lax.rsqrt(ms + LN_EPS) * (1.0 - lambda_init))
    o_ref[...] = (ot.T * sw_ref[...]).astype(o_ref.dtype)


def diff_attention(q_t, k_r, v_t, lq1, lk1, lq2, lk2, subln_w, lambda_init, *, tq=ATT_TQ):
    s = k_r.shape[0]
    nkb, tk = v_t.shape[1], v_t.shape[3]
    tq = min(tq, s)
    row = lambda v: v.reshape(1, -1)
    const = lambda n: pl.BlockSpec((1, n), lambda h, i: (0, 0))
    return pl.pallas_call(
        functools.partial(_attn_body, lambda_init=lambda_init),
        grid=(DA_HEADS, s // tq),
        in_specs=[pl.BlockSpec((LANES, tq), lambda h, i: (h, i)),
                  pl.BlockSpec((s, LANES), lambda h, i: (0, h)),
                  pl.BlockSpec((None, nkb, DA_V_DIM, tk), lambda h, i: (h, 0, 0, 0)),
                  const(DA_HEAD_DIM), const(DA_HEAD_DIM), const(DA_HEAD_DIM), const(DA_HEAD_DIM),
                  const(DA_V_DIM)],
        out_specs=pl.BlockSpec((tq, LANES), lambda h, i: (i, h)),
        out_shape=jax.ShapeDtypeStruct((s, DA_HEADS * DA_V_DIM), BF16),
        scratch_shapes=[pltpu.VMEM((2, 1, tq), F32), pltpu.VMEM((2, 1, tq), F32),
                        pltpu.VMEM((2, DA_V_DIM, tq), F32),
                        pltpu.VMEM((2, tk, tq), F32), pltpu.VMEM((2, tk, tq), F32),
                        pltpu.VMEM((2, 1, tq), F32), pltpu.VMEM((2, 1, tq), F32)],
        compiler_params=_cparams(("arbitrary", "arbitrary")), name="diff_attention",
    )(q_t, k_r, v_t, row(lq1), row(lk1), row(lq2), row(lk2), row(subln_w))


def _merge_body(ha_ref, hb_ref, hc_ref, wa_ref, wb_ref, wc_ref, ga_ref, gb_ref, gc_ref, bias_ref, o_ref):
    def gated(h_ref, w_ref, g_ref, bi):
        y = jnp.dot(h_ref[...], w_ref[...], preferred_element_type=F32)
        return jax.nn.sigmoid(g_ref[...].astype(F32) + bias_ref[bi:bi + 1, :]) * y

    m = gated(ha_ref, wa_ref, ga_ref, 0) + gated(hb_ref, wb_ref, gb_ref, 1) + gated(hc_ref, wc_ref, gc_ref, 2)
    o_ref[...] = m.astype(o_ref.dtype)


def gated_merge(h_a, h_b, h_c, w_a, w_b, w_c, u, gate_b, *, tm=1024, tn=512):
    s = h_a.shape[0]
    d = w_a.shape[1]
    tm = min(tm, s)
    gblk = C_GATE // tn
    nblk = d // tn
    return pl.pallas_call(
        _merge_body,
        grid=(s // tm, d // tn),
        in_specs=[pl.BlockSpec((tm, h_a.shape[1]), lambda i, j: (i, 0)),
                  pl.BlockSpec((tm, h_b.shape[1]), lambda i, j: (i, 0)),
                  pl.BlockSpec((tm, h_c.shape[1]), lambda i, j: (i, 0)),
                  pl.BlockSpec((w_a.shape[0], tn), lambda i, j: (0, j)),
                  pl.BlockSpec((w_b.shape[0], tn), lambda i, j: (0, j)),
                  pl.BlockSpec((w_c.shape[0], tn), lambda i, j: (0, j)),
                  pl.BlockSpec((tm, tn), lambda i, j: (i, gblk + j)),
                  pl.BlockSpec((tm, tn), lambda i, j: (i, gblk + nblk + j)),
                  pl.BlockSpec((tm, tn), lambda i, j: (i, gblk + 2 * nblk + j)),
                  pl.BlockSpec((3, tn), lambda i, j: (0, j))],
        out_specs=pl.BlockSpec((tm, tn), lambda i, j: (i, j)),
        out_shape=jax.ShapeDtypeStruct((s, d), BF16),
        compiler_params=_cparams(("arbitrary", "arbitrary")), name="gated_merge",
    )(h_a, h_b, h_c, w_a, w_b, w_c, u, u, u, gate_b)


def _residual_matmul_body(x_ref, a_ref, w_ref, o_ref):
    o_ref[...] = x_ref[...] + jnp.dot(a_ref[...], w_ref[...], preferred_element_type=F32)


def residual_matmul(x, a, w, *, tm=512):
    s, d = x.shape
    tm = min(tm, s)
    return pl.pallas_call(
        _residual_matmul_body,
        grid=(s // tm,),
        in_specs=[pl.BlockSpec((tm, d), lambda i: (i, 0)),
                  pl.BlockSpec((tm, a.shape[1]), lambda i: (i, 0)),
                  pl.BlockSpec(w.shape, lambda i: (0, 0))],
        out_specs=pl.BlockSpec((tm, d), lambda i: (i, 0)),
        out_shape=jax.ShapeDtypeStruct((s, d), F32),
        compiler_params=_cparams(("arbitrary",)), name="residual_matmul",
    )(x, a, w)


FFN_HALO = 16


def _ffn_up_body(x_ref, xh_ref, nw_ref, wg_ref, wu_ref, cw_ref, o_ref, xn_ref, xnh_ref, ebuf, *, row_chunk):
    tm = x_ref.shape[0]
    i = pl.program_id(0)

    def normed(x):
        ms = jnp.mean(x * x, axis=-1, keepdims=True)
        return (x * lax.rsqrt(ms + RMS_EPS) * nw_ref[...]).astype(BF16)

    @pl.when(pl.program_id(1) == 0)
    def _():
        def chunk(c, carry):
            r0 = pl.multiple_of(c * row_chunk, row_chunk)
            xn_ref[pl.ds(r0, row_chunk), :] = normed(x_ref[pl.ds(r0, row_chunk), :])
            return carry
        lax.fori_loop(0, tm // row_chunk, chunk, 0)
        xnh_ref[...] = normed(xh_ref[...])

    gate_h = jnp.dot(xnh_ref[...], wg_ref[...], preferred_element_type=F32)
    ebuf[0:FFN_HALO, :] = jnp.where(i > 0, gate_h, 0.0)
    ebuf[FFN_HALO:, :] = jnp.dot(xn_ref[...], wg_ref[...], preferred_element_type=F32)
    up = jnp.dot(xn_ref[...], wu_ref[...], preferred_element_type=F32)
    base = FFN_HALO - (FFN_CONV_K - 1)
    acc = jnp.zeros(o_ref.shape, F32)
    for t in range(FFN_CONV_K):
        acc = acc + cw_ref[t:t + 1, :] * ebuf[base + t: base + t + tm, :]
    o_ref[...] = (_silu(acc) * up).astype(o_ref.dtype)


def ffn_up_act(x, nw, w_up, conv_w, *, tm=1024, tn=512):
    s, d = x.shape
    tm = min(tm, s)
    nj = D_FF // tn
    hb = tm // FFN_HALO
    halo_idx = lambda i: jnp.maximum(i * hb - 1, 0)
    cw = jnp.zeros((8, D_FF), F32).at[:FFN_CONV_K].set(conv_w)
    return pl.pallas_call(
        functools.partial(_ffn_up_body, row_chunk=min(128, tm)),
        grid=(s // tm, nj),
        in_specs=[pl.BlockSpec((tm, d), lambda i, j: (i, 0)),
                  pl.BlockSpec((FFN_HALO, d), lambda i, j: (halo_idx(i), 0)),
                  pl.BlockSpec((1, d), lambda i, j: (0, 0)),
                  pl.BlockSpec((d, tn), lambda i, j: (0, j)),
                  pl.BlockSpec((d, tn), lambda i, j: (0, nj + j)),
                  pl.BlockSpec((8, tn), lambda i, j: (0, j))],
        out_specs=pl.BlockSpec((tm, tn), lambda i, j: (i, j)),
        out_shape=jax.ShapeDtypeStruct((s, D_FF), BF16),
        scratch_shapes=[pltpu.VMEM((tm, d), BF16), pltpu.VMEM((FFN_HALO, d), BF16),
                        pltpu.VMEM((FFN_HALO + tm, tn), F32)],
        compiler_params=_cparams(("arbitrary", "arbitrary")), name="ffn_up_act",
    )(x, x, nw.reshape(1, d), w_up, w_up, cw)


def _ffn_down_body(x_ref, a_ref, wd_ref, fw_ref, o_ref, *, final_norm):
    k = pl.program_id(1)
    contrib = jnp.dot(a_ref[...], wd_ref[...], preferred_element_type=F32)

    @pl.when(k == 0)
    def _():
        o_ref[...] = x_ref[...] + contrib

    @pl.when(k > 0)
    def _():
        o_ref[...] += contrib

    if final_norm:
        @pl.when(k == pl.num_programs(1) - 1)
        def _():
            y = o_ref[...]
            ms = jnp.mean(y * y, axis=-1, keepdims=True)
            o_ref[...] = y * lax.rsqrt(ms + RMS_EPS) * fw_ref[...]


def ffn_down(x, act, w_down, final_w, *, final_norm, tm=512, tk=1408):
    s, d = x.shape
    tm = min(tm, s)
    nk = D_FF // tk
    return pl.pallas_call(
        functools.partial(_ffn_down_body, final_norm=final_norm),
        grid=(s // tm, nk),
        in_specs=[pl.BlockSpec((tm, d), lambda i, k: (i, 0)),
                  pl.BlockSpec((tm, tk), lambda i, k: (i, k)),
                  pl.BlockSpec((tk, d), lambda i, k: (k, 0)),
                  pl.BlockSpec((1, d), lambda i, k: (0, 0))],
        out_specs=pl.BlockSpec((tm, d), lambda i, k: (i, 0)),
        out_shape=jax.ShapeDtypeStruct((s, d), F32),
        compiler_params=_cparams(("arbitrary", "arbitrary")), name="ffn_down",
    )(x, act, w_down, final_w.reshape(1, d))


def kernel(x, positions, norm1_w, w_in, gate_b, conv_dw_w, conv_dw_b, conv_ln_w, conv_ln_b, conv_out_w,
           ssd_conv_w, ssd_conv_b, ssd_dt_bias, ssd_a_log, ssd_d, ssd_norm_w, ssd_out_w,
           da_lambda_q1, da_lambda_k1, da_lambda_q2, da_lambda_k2, da_subln_w, da_out_w,
           w_o, norm2_w, ffn_up_w, ffn_dw_w, ffn_down_w, final_norm_w):
    bsz, s_len, d = x.shape
    assert bsz == 1 and d == D_MODEL
    xc = x.reshape(s_len, d)
    pos = positions.reshape(s_len)
    for l in range(DEPTH):
        lambda_init = 0.8 - 0.6 * math.exp(-0.3 * l)
        w_main = jnp.concatenate([w_in[l][:, :DT_OFF], w_in[l][:, DT_OFF + SSD_HEADS:]], axis=1).astype(BF16)
        w_dt = jnp.zeros((d, DT_PAD), BF16).at[:, :SSD_HEADS].set(
            w_in[l][:, DT_OFF:DT_OFF + SSD_HEADS].astype(BF16))
        u, dt_raw = norm_matmul(xc, norm1_w[l], w_main, w_dt)
        h_a = conformer_pre(u, conv_dw_w[l], conv_dw_b[l], conv_ln_w[l], conv_ln_b[l])
        h_b = ssd_pre(u, dt_raw, ssd_conv_w[l], ssd_conv_b[l], ssd_dt_bias[l], ssd_a_log[l],
                      ssd_d[l], ssd_norm_w[l])
        q_t, k_r, v_t = rope_prep(u, pos)
        h_c = diff_attention(q_t, k_r, v_t, da_lambda_q1[l], da_lambda_k1[l], da_lambda_q2[l],
                             da_lambda_k2[l], da_subln_w[l], lambda_init)
        merged = gated_merge(h_a, h_b, h_c, conv_out_w[l].astype(BF16), ssd_out_w[l].astype(BF16),
                             da_out_w[l].astype(BF16), u, gate_b[l])
        xc = residual_matmul(xc, merged, w_o[l].astype(BF16))
        act = ffn_up_act(xc, norm2_w[l], ffn_up_w[l].astype(BF16), ffn_dw_w[l])
        xc = ffn_down(xc, act, ffn_down_w[l].astype(BF16), final_norm_w, final_norm=(l == DEPTH - 1))
    return xc.reshape(bsz, s_len, d)
```

```python
import functools
import math

import jax
import jax.numpy as jnp
from jax import lax
from jax.experimental import pallas as pl
from jax.experimental.pallas import tpu as pltpu

F32 = jnp.float32
BF16 = jnp.bfloat16

D_MODEL = 2048
DEPTH = 2
CONV_CH = 1024
CONV_K = 31
SSD_HEADS = 32
SSD_HEADDIM = 64
SSD_INNER = SSD_HEADS * SSD_HEADDIM
SSD_GROUPS = 4
SSD_STATE = 128
SSD_CONV_K = 4
SSD_CHUNK = 128
SSD_GROUP_W = SSD_INNER // SSD_GROUPS
DA_HEADS = 8
DA_HEAD_DIM = 64
DA_V_DIM = 128
ROPE_DIM = 16
ROPE_THETA = 500000.0
D_FF = 5632
FFN_CONV_K = 3
RMS_EPS = 1e-6
LN_EPS = 1e-5
NEG_INF = -1e30

C_CONV_A = 0
C_CONV_G = 1024
C_Z = 2048
C_X = 4096
C_B = 6144
C_C = 6656
C_Q = 7168
C_K = 8192
C_V = 9216
C_GATE = 10240
U_WIDTH = 16384
DT_OFF = 7168
DT_PAD = 128

LANES = 128
VMEM_LIMIT = 56 * 1024 * 1024


def _cparams(sem, flags=None):
    return pltpu.CompilerParams(dimension_semantics=sem, vmem_limit_bytes=VMEM_LIMIT, flags=flags)


def _in_proj_body(x_ref, nw_ref, wh_ref, wt_ref, ws_ref, o_ref, os_ref, xn_ref, *, n_head_blocks, row_chunk):
    tm = x_ref.shape[0]
    j = pl.program_id(1)

    @pl.when(j == 0)
    def _():
        def chunk(c, carry):
            r0 = pl.multiple_of(c * row_chunk, row_chunk)
            x = x_ref[pl.ds(r0, row_chunk), :]
            ms = jnp.mean(x * x, axis=-1, keepdims=True)
            xn_ref[pl.ds(r0, row_chunk), :] = (x * lax.rsqrt(ms + RMS_EPS) * nw_ref[...]).astype(BF16)
            return carry
        lax.fori_loop(0, tm // row_chunk, chunk, 0)
        os_ref[...] = jnp.dot(xn_ref[...], ws_ref[...], preferred_element_type=F32)

    @pl.when(j < n_head_blocks)
    def _():
        o_ref[...] = jnp.dot(xn_ref[...], wh_ref[...], preferred_element_type=F32).astype(o_ref.dtype)

    @pl.when(j >= n_head_blocks)
    def _():
        o_ref[...] = jnp.dot(xn_ref[...], wt_ref[...], preferred_element_type=F32).astype(o_ref.dtype)


def in_proj(x, nw, w_all, w_tail, w_side, layer, *, tm=1024, tn=1024):
    s, d = x.shape
    tm = min(tm, s)
    nh, nt = DT_OFF // tn, w_tail.shape[2] // tn
    ns = w_side.shape[2]
    return pl.pallas_call(
        functools.partial(_in_proj_body, n_head_blocks=nh, row_chunk=min(128, tm)),
        grid=(s // tm, nh + nt),
        in_specs=[pl.BlockSpec((tm, d), lambda i, j: (i, 0)),
                  pl.BlockSpec((1, d), lambda i, j: (0, 0)),
                  pl.BlockSpec((None, d, tn), lambda i, j: (layer, 0, jnp.minimum(j, nh - 1))),
                  pl.BlockSpec((None, d, tn), lambda i, j: (layer, 0, jnp.maximum(j - nh, 0))),
                  pl.BlockSpec((None, d, ns), lambda i, j: (layer, 0, 0))],
        out_specs=[pl.BlockSpec((tm, tn), lambda i, j: (i, j)),
                   pl.BlockSpec((tm, ns), lambda i, j: (i, 0))],
        out_shape=[jax.ShapeDtypeStruct((s, (nh + nt) * tn), BF16), jax.ShapeDtypeStruct((s, ns), F32)],
        scratch_shapes=[pltpu.VMEM((tm, d), BF16)],
        compiler_params=_cparams(("arbitrary", "arbitrary")), name="in_proj",
    )(x, nw.reshape(1, d), w_all, w_tail, w_side)


CONV_HALO = 32
CONV_ROWS = 128


def _conformer_body(a_ref, g_ref, ah_ref, gh_ref, w_ref, b_ref, lnw_ref, lnb_ref, o_ref, hbuf, cbuf):
    ts = a_ref.shape[0]
    i = pl.program_id(0)
    halo = ah_ref[...].astype(F32) * jax.nn.sigmoid(gh_ref[...].astype(F32))
    hbuf[0:CONV_HALO, :] = jnp.where(i > 0, halo, 0.0)
    hbuf[CONV_HALO:, :] = a_ref[...].astype(F32) * jax.nn.sigmoid(g_ref[...].astype(F32))
    base = CONV_HALO - (CONV_K - 1)
    sub = 8
    rows = CONV_ROWS
    ext = rows + CONV_HALO
    for c in range(CONV_CH // LANES):
        cs = slice(c * LANES, (c + 1) * LANES)
        for r0 in range(0, ts, rows):
            slab = hbuf[r0:r0 + ext, cs]
            acc = jnp.zeros((rows, LANES), F32)
            for res in range(sub):
                taps = [k for k in range(CONV_K) if (base + k) % sub == res]
                if not taps:
                    continue
                rolled = slab if res == 0 else pltpu.roll(slab, ext - res, axis=0)
                for k in taps:
                    a = (base + k) // sub * sub
                    acc = acc + w_ref[k:k + 1, cs] * rolled[a:a + rows]
            cbuf[r0:r0 + rows, cs] = acc + b_ref[:, cs]
    y = cbuf[...]
    mu = jnp.mean(y, axis=-1, keepdims=True)
    yc = y - mu
    var = jnp.mean(yc * yc, axis=-1, keepdims=True)
    yn = yc * lax.rsqrt(var + LN_EPS) * lnw_ref[...] + lnb_ref[...]
    o_ref[...] = (yn * jax.nn.sigmoid(yn)).astype(o_ref.dtype)


def conformer_pre(u, w_dw, b_dw, ln_w, ln_b, *, ts=256):
    s = u.shape[0]
    ts = min(ts, s)
    hb = ts // CONV_HALO
    w_pad = jnp.zeros((32, CONV_CH), F32).at[:CONV_K].set(w_dw)
    row = lambda v: v.reshape(1, CONV_CH)
    halo_idx = lambda i: jnp.maximum(i * hb - 1, 0)
    return pl.pallas_call(
        _conformer_body,
        grid=(s // ts,),
        in_specs=[pl.BlockSpec((ts, CONV_CH), lambda i: (i, C_CONV_A // CONV_CH)),
                  pl.BlockSpec((ts, CONV_CH), lambda i: (i, C_CONV_G // CONV_CH)),
                  pl.BlockSpec((CONV_HALO, CONV_CH), lambda i: (halo_idx(i), C_CONV_A // CONV_CH)),
                  pl.BlockSpec((CONV_HALO, CONV_CH), lambda i: (halo_idx(i), C_CONV_G // CONV_CH)),
                  pl.BlockSpec((32, CONV_CH), lambda i: (0, 0)),
                  pl.BlockSpec((1, CONV_CH), lambda i: (0, 0)),
                  pl.BlockSpec((1, CONV_CH), lambda i: (0, 0)),
                  pl.BlockSpec((1, CONV_CH), lambda i: (0, 0))],
        out_specs=pl.BlockSpec((ts, CONV_CH), lambda i: (i, 0)),
        out_shape=jax.ShapeDtypeStruct((s, CONV_CH), BF16),
        scratch_shapes=[pltpu.VMEM((CONV_HALO + ts, CONV_CH), F32), pltpu.VMEM((ts, CONV_CH), F32)],
        compiler_params=_cparams(("arbitrary",)), name="conformer_pre",
    )(u, u, u, u, w_pad, row(b_dw), row(ln_w), row(ln_b))


SSD_HALO = 8
SSD_CONV_ROWS = 64
SSD_CONV_LANES = 256


def _split3(v):
    hi = v.astype(BF16)
    r1 = v - hi.astype(F32)
    mid = r1.astype(BF16)
    lo = (r1 - mid.astype(F32)).astype(BF16)
    return hi, mid, lo


def _dot01_left(m01, v):
    hi, mid, lo = _split3(v)
    d = lambda p: jnp.dot(m01, p, preferred_element_type=F32)
    return d(hi) + d(mid) + d(lo)


def _softplus(x):
    return jnp.maximum(x, 0.0) + jnp.log1p(jnp.exp(-jnp.abs(x)))


def _silu(x):
    return x * jax.nn.sigmoid(x)


def _ssd_body(x_ref, b_ref, c_ref, z_ref, xh_ref, bh_ref, ch_ref, dt_ref,
              cwx_ref, cwb_ref, cwc_ref, cbx_ref, cbb_ref, cbc_ref,
              dtb_ref, alog_ref, dx_ref, nw_ref, ex_ref,
              o_ref, state, ebuf, xs_c, b_c, c_c, ybuf, dt_s, cum_s, cx_s, dtx_s):
    ts = x_ref.shape[0]
    L = SSD_CHUNK
    i = pl.program_id(0)

    @pl.when(i == 0)
    def _():
        state[...] = jnp.zeros_like(state)

    def conv(in_ref, halo_ref, w_ref, bias_ref, out_ref):
        width = in_ref.shape[1]
        ebuf[0:SSD_HALO, 0:width] = jnp.where(i > 0, halo_ref[...].astype(F32), 0.0)
        ebuf[SSD_HALO:, 0:width] = in_ref[...].astype(F32)
        base = SSD_HALO - (SSD_CONV_K - 1)
        rb, cw = SSD_CONV_ROWS, SSD_CONV_LANES
        ext = rb + SSD_HALO

        def row_block(bi, carry):
            r0 = pl.multiple_of(bi * rb, rb)
            for c in range(width // cw):
                cs = slice(c * cw, (c + 1) * cw)
                slab = ebuf[pl.ds(r0, ext), cs]
                acc = bias_ref[:, cs]
                for k in range(SSD_CONV_K):
                    a, res = (base + k) // 8 * 8, (base + k) % 8
                    rolled = slab if res == 0 else pltpu.roll(slab, ext - res, axis=0)
                    acc = acc + w_ref[k:k + 1, cs] * rolled[a:a + rb]
                out_ref[pl.ds(r0, rb), cs] = _silu(acc)
            return carry

        lax.fori_loop(0, ts // rb, row_block, 0)

    conv(x_ref, xh_ref, cwx_ref, cbx_ref, xs_c)
    conv(b_ref, bh_ref, cwb_ref, cbb_ref, b_c)
    conv(c_ref, ch_ref, cwc_ref, cbc_ref, c_c)

    a_row = -jnp.exp(alog_ref[...])
    rr = lax.broadcasted_iota(jnp.int32, (L, L), 0)
    cc = lax.broadcasted_iota(jnp.int32, (L, L), 1)
    tri = rr >= cc
    tri01 = jnp.where(tri, 1.0, 0.0).astype(BF16)
    lane_lo = lax.broadcasted_iota(jnp.int32, (L, LANES), 1) < SSD_HEADDIM

    dt_all = _softplus(dt_ref[...] + dtb_ref[...])
    dt_s[...] = dt_all
    for ci in range(ts // L):
        cum_s[ci * L:(ci + 1) * L, :] = _dot01_left(tri01, dt_all[ci * L:(ci + 1) * L] * a_row)
    for src, dst in ((cum_s, cx_s), (dt_s, dtx_s)):
        dst[...] = jnp.dot(jnp.concatenate(_split3(src[...]), axis=1), ex_ref[...],
                           preferred_element_type=F32)

    def chunk(ci, carry):
        r0 = pl.multiple_of(ci * L, L)
        rows = pl.ds(r0, L)
        dt = dt_s[rows, :]
        cum = cum_s[rows, :]
        cum_t = cum.T
        dt_t = dt.T
        cum_x = cx_s[rows, :]
        dt_x = dtx_s[rows, :]
        cum_last = cum_x[L - 1:L, :]
        e_x = jnp.exp(cum_x)
        xs = xs_c[rows, :]
        xd = (xs * jnp.exp(cum_last - cum_x) * dt_x).astype(BF16)
        e_last = jnp.exp(cum_last)
        for g in range(SSD_GROUPS):
            gs = slice(g * SSD_GROUP_W, (g + 1) * SSD_GROUP_W)
            ns = slice(g * SSD_STATE, (g + 1) * SSD_STATE)
            bg = b_c[rows, ns]
            cg = c_c[rows, ns].astype(BF16)
            cb = lax.dot_general(cg, bg.astype(BF16), (((1,), (1,)), ((), ())),
                                 preferred_element_type=F32)
            st = state[:, gs]
            y_off = jnp.dot(cg, st.astype(BF16), preferred_element_type=F32) * e_x[:, gs]
            state[:, gs] = e_last[:, gs] * st + jnp.dot(bg.T.astype(BF16), xd[:, gs],
                                                        preferred_element_type=F32)
            for pr in range(SSD_GROUP_W // LANES):
                h0 = g * (SSD_HEADS // SSD_GROUPS) + 2 * pr
                ps = slice(h0 * SSD_HEADDIM, h0 * SSD_HEADDIM + LANES)
                x_pair = xs[:, ps].astype(BF16)
                ys = []
                for h in (h0, h0 + 1):
                    seg = cum[:, h:h + 1] - cum_t[h:h + 1, :]
                    decay = jnp.where(tri, jnp.exp(jnp.minimum(seg, 0.0)), 0.0)
                    wts = (cb * decay * dt_t[h:h + 1, :]).astype(BF16)
                    ys.append(jnp.dot(wts, x_pair, preferred_element_type=F32))
                y_pair = jnp.where(lane_lo, ys[0], ys[1])
                ybuf[:, ps] = y_pair + y_off[:, pr * LANES:(pr + 1) * LANES]
        y = ybuf[...] + xs * dx_ref[...]
        y = y * _silu(z_ref[rows, :].astype(F32))
        for g in range(SSD_GROUPS):
            gs = slice(g * SSD_GROUP_W, (g + 1) * SSD_GROUP_W)
            yg = y[:, gs]
            ms = jnp.mean(yg * yg, axis=-1, keepdims=True)
            o_ref[rows, gs] = (yg * lax.rsqrt(ms + RMS_EPS) * nw_ref[:, gs]).astype(o_ref.dtype)
        return carry

    lax.fori_loop(0, ts // L, chunk, 0)


def ssd_pre(u, dt_raw, conv_w, conv_b, dt_bias, a_log, d_skip, norm_w, *, ts=512):
    s = u.shape[0]
    ts = min(ts, s)
    hb = ts // SSD_HALO
    halo_idx = lambda i: jnp.maximum(i * hb - 1, 0)
    cw = jnp.zeros((8, conv_w.shape[1]), F32).at[:SSD_CONV_K].set(conv_w)
    n_bc = SSD_GROUPS * SSD_STATE
    cwx, cwb, cwc = cw[:, :SSD_INNER], cw[:, SSD_INNER:SSD_INNER + n_bc], cw[:, SSD_INNER + n_bc:]
    cb2 = conv_b.reshape(1, -1)
    cbx, cbb, cbc = cb2[:, :SSD_INNER], cb2[:, SSD_INNER:SSD_INNER + n_bc], cb2[:, SSD_INNER + n_bc:]
    pad_h = lambda v: jnp.zeros((1, DT_PAD), F32).at[0, :SSD_HEADS].set(v)
    dx = jnp.repeat(d_skip, SSD_HEADDIM).reshape(1, SSD_INNER)
    ex = (jnp.arange(SSD_INNER)[None, :] // SSD_HEADDIM == jnp.arange(DT_PAD)[:, None]).astype(BF16)
    ex = jnp.concatenate([ex, ex, ex], axis=0)
    const = lambda shape: pl.BlockSpec(shape, lambda i: (0, 0))
    return pl.pallas_call(
        _ssd_body,
        grid=(s // ts,),
        in_specs=[pl.BlockSpec((ts, SSD_INNER), lambda i: (i, C_X // SSD_INNER)),
                  pl.BlockSpec((ts, n_bc), lambda i: (i, C_B // n_bc)),
                  pl.BlockSpec((ts, n_bc), lambda i: (i, C_C // n_bc)),
                  pl.BlockSpec((ts, SSD_INNER), lambda i: (i, C_Z // SSD_INNER)),
                  pl.BlockSpec((SSD_HALO, SSD_INNER), lambda i: (halo_idx(i), C_X // SSD_INNER)),
                  pl.BlockSpec((SSD_HALO, n_bc), lambda i: (halo_idx(i), C_B // n_bc)),
                  pl.BlockSpec((SSD_HALO, n_bc), lambda i: (halo_idx(i), C_C // n_bc)),
                  pl.BlockSpec((ts, DT_PAD), lambda i: (i, 0)),
                  const((8, SSD_INNER)), const((8, n_bc)), const((8, n_bc)),
                  const((1, SSD_INNER)), const((1, n_bc)), const((1, n_bc)),
                  const((1, DT_PAD)), const((1, DT_PAD)), const((1, SSD_INNER)), const((1, SSD_INNER)),
                  const((3 * DT_PAD, SSD_INNER))],
        out_specs=pl.BlockSpec((ts, SSD_INNER), lambda i: (i, 0)),
        out_shape=jax.ShapeDtypeStruct((s, SSD_INNER), BF16),
        scratch_shapes=[pltpu.VMEM((SSD_STATE, SSD_INNER), F32),
                        pltpu.VMEM((SSD_HALO + ts, SSD_INNER), F32),
                        pltpu.VMEM((ts, SSD_INNER), F32),
                        pltpu.VMEM((ts, n_bc), F32),
                        pltpu.VMEM((ts, n_bc), F32),
                        pltpu.VMEM((SSD_CHUNK, SSD_INNER), F32),
                        pltpu.VMEM((ts, DT_PAD), F32), pltpu.VMEM((ts, DT_PAD), F32),
                        pltpu.VMEM((ts, SSD_INNER), F32), pltpu.VMEM((ts, SSD_INNER), F32)],
        compiler_params=_cparams(("arbitrary",)), name="ssd_pre",
    )(u, u, u, u, u, u, u, dt_raw, cwx, cwb, cwc, cbx, cbb, cbc,
      pad_h(dt_bias), pad_h(a_log), dx, norm_w.reshape(1, SSD_INNER), ex)


LOG2E = 1.4426950408889634
ATT_TQ = 1024
ATT_TK = ATT_TQ // 2
ATT_V_ROWS = DA_V_DIM + 16


def _rope_body(q_ref, k_ref, v_ref, pos_ref, inv_ref, qo_ref, ko_ref, vo_ref):
    ts = q_ref.shape[0]
    lane = lax.broadcasted_iota(jnp.int32, (ts, LANES), 1) % DA_HEAD_DIM
    half = ROPE_DIM // 2
    ang = pos_ref[...].astype(F32) * inv_ref[...]
    cos = jnp.cos(ang)
    sin = jnp.sin(ang)
    c_all = jnp.where(lane < ROPE_DIM, cos, 1.0)
    s_lo = jnp.where(lane < half, -sin, 0.0)
    s_hi = jnp.where((lane >= half) & (lane < ROPE_DIM), sin, 0.0)
    q_scale = DA_HEAD_DIM ** -0.5 * LOG2E

    def rotate(t):
        up = pltpu.roll(t, LANES - half, axis=1)
        dn = pltpu.roll(t, half, axis=1)
        return t * c_all + up * s_lo + dn * s_hi

    for c in range(q_ref.shape[1] // LANES):
        cs = slice(c * LANES, (c + 1) * LANES)
        ko_ref[:, cs] = rotate(k_ref[:, cs].astype(F32)).astype(ko_ref.dtype)
        qo_ref[cs, :] = (rotate(q_ref[:, cs].astype(F32)) * q_scale).T.astype(qo_ref.dtype)
        vo_ref[c, 0:DA_V_DIM, :] = v_ref[:, cs].astype(F32).T.astype(vo_ref.dtype)
        extra = lax.broadcasted_iota(jnp.int32, (ATT_V_ROWS - DA_V_DIM, ts), 0) == 0
        vo_ref[c, DA_V_DIM:ATT_V_ROWS, :] = jnp.where(extra, 1.0, 0.0).astype(vo_ref.dtype)


def rope_prep(u, positions, *, ts=ATT_TK):
    s = u.shape[0]
    ts = min(ts, s)
    w = DA_HEADS * 2 * DA_HEAD_DIM
    inv = 1.0 / (ROPE_THETA ** (jnp.arange(0, ROPE_DIM, 2, dtype=F32) / ROPE_DIM))
    lane = jnp.arange(LANES) % DA_HEAD_DIM
    inv_lane = jnp.where(lane < ROPE_DIM, inv[lane % (ROPE_DIM // 2)], 0.0).reshape(1, LANES)
    return pl.pallas_call(
        _rope_body,
        grid=(s // ts,),
        in_specs=[pl.BlockSpec((ts, w), lambda i: (i, C_Q // w)),
                  pl.BlockSpec((ts, w), lambda i: (i, C_K // w)),
                  pl.BlockSpec((ts, w), lambda i: (i, C_V // w)),
                  pl.BlockSpec((ts, 1), lambda i: (i, 0)),
                  pl.BlockSpec((1, LANES), lambda i: (0, 0))],
        out_specs=[pl.BlockSpec((w, ts), lambda i: (0, i)), pl.BlockSpec((ts, w), lambda i: (i, 0)),
                   pl.BlockSpec((DA_HEADS, None, ATT_V_ROWS, ts), lambda i: (0, i, 0, 0))],
        out_shape=[jax.ShapeDtypeStruct((w, s), BF16), jax.ShapeDtypeStruct((s, w), BF16),
                   jax.ShapeDtypeStruct((DA_HEADS, s // ts, ATT_V_ROWS, ts), BF16)],
        compiler_params=_cparams(("arbitrary",)), name="rope_prep",
    )(u, u, u, positions.reshape(s, 1), inv_lane)


def _attn_body(qt_ref, k_ref, vt_ref, lq1_ref, lk1_ref, lq2_ref, lk2_ref, sw_ref, o_ref,
               m_ref, acc_ref, sa_ref, sb_ref, ma_ref, mb_ref, *, lambda_init):
    tq = qt_ref.shape[1]
    tk = vt_ref.shape[2]
    assert tq == 2 * tk
    i = pl.program_id(1)
    row = lax.broadcasted_iota(jnp.int32, (LANES, tq), 0)
    qt = qt_ref[...]
    zero = jnp.zeros_like(qt)
    qc = (jnp.where(row < DA_HEAD_DIM, qt, zero), jnp.where(row >= DA_HEAD_DIM, qt, zero))
    m_ref[...] = jnp.full(m_ref.shape, NEG_INF, F32)
    acc_ref[...] = jnp.zeros(acc_ref.shape, F32)

    def scores(j, s_ref, mc_ref):
        kb = k_ref[pl.ds(pl.multiple_of(j * tk, tk), tk), :]
        for c in range(2):
            s = jnp.dot(kb, qc[c], preferred_element_type=F32)
            s_ref[c] = s
            mc_ref[c] = jnp.max(s, axis=0, keepdims=True)

    def consume(j, s_ref, mc_ref, masked):
        vtb = vt_ref[j]
        for c in range(2):
            s = s_ref[c]
            if masked:
                kk = lax.broadcasted_iota(jnp.int32, (tk, tq), 0) + (j * tk - i * tq)
                qq = lax.broadcasted_iota(jnp.int32, (tk, tq), 1)
                s = jnp.where(kk <= qq, s, NEG_INF)
                m_cur = jnp.max(s, axis=0, keepdims=True)
            else:
                m_cur = mc_ref[c]
            m_prev = m_ref[c]
            m_new = jnp.maximum(m_prev, m_cur)
            alpha = jnp.exp2(m_prev - m_new)
            p = jnp.exp2(s - m_new).astype(BF16)
            acc_ref[c] = alpha * acc_ref[c] + jnp.dot(vtb, p, preferred_element_type=F32)
            m_ref[c] = m_new

    def pair(t, carry):
        j0 = 2 * t
        scores(j0 + 1, sb_ref, mb_ref)
        consume(j0, sa_ref, ma_ref, False)
        scores(j0 + 2, sa_ref, ma_ref)
        consume(j0 + 1, sb_ref, mb_ref, False)
        return carry

    scores(0, sa_ref, ma_ref)
    lax.fori_loop(0, i, pair, 0)
    scores(2 * i + 1, sb_ref, mb_ref)
    consume(2 * i, sa_ref, ma_ref, True)
    consume(2 * i + 1, sb_ref, mb_ref, True)

    lam = (jnp.exp(jnp.sum(lq1_ref[...] * lk1_ref[...], axis=-1, keepdims=True))
           - jnp.exp(jnp.sum(lq2_ref[...] * lk2_ref[...], axis=-1, keepdims=True)) + lambda_init)
    nd = DA_V_DIM
    ot = (acc_ref[0, 0:nd, :] / acc_ref[0, nd:nd + 1, :]
          - lam * (acc_ref[1, 0:nd, :] / acc_ref[1, nd:nd + 1, :]))
    ms = jnp.mean(ot * ot, axis=0, keepdims=True)
    ot = ot * (lax.rsqrt(ms + LN_EPS) * (1.0 - lambda_init))
    o_ref[...] = (ot.T * sw_ref[...]).astype(o_ref.dtype)


def diff_attention(q_t, k_r, v_t, lq1, lk1, lq2, lk2, subln_w, lambda_init, *, tq=ATT_TQ):
    s = k_r.shape[0]
    nkb, tk = v_t.shape[1], v_t.shape[3]
    tq = min(tq, s)
    row = lambda v: v.reshape(1, -1)
    const = lambda n: pl.BlockSpec((1, n), lambda h, i: (0, 0))
    return pl.pallas_call(
        functools.partial(_attn_body, lambda_init=lambda_init),
        grid=(DA_HEADS, s // tq),
        in_specs=[pl.BlockSpec((LANES, tq), lambda h, i: (h, i)),
                  pl.BlockSpec((s, LANES), lambda h, i: (0, h)),
                  pl.BlockSpec((None, nkb, ATT_V_ROWS, tk), lambda h, i: (h, 0, 0, 0)),
                  const(DA_HEAD_DIM), const(DA_HEAD_DIM), const(DA_HEAD_DIM), const(DA_HEAD_DIM),
                  const(DA_V_DIM)],
        out_specs=pl.BlockSpec((tq, LANES), lambda h, i: (i, h)),
        out_shape=jax.ShapeDtypeStruct((s, DA_HEADS * DA_V_DIM), BF16),
        scratch_shapes=[pltpu.VMEM((2, 1, tq), F32),
                        pltpu.VMEM((2, ATT_V_ROWS, tq), F32),
                        pltpu.VMEM((2, tk, tq), F32), pltpu.VMEM((2, tk, tq), F32),
                        pltpu.VMEM((2, 1, tq), F32), pltpu.VMEM((2, 1, tq), F32)],
        compiler_params=_cparams(("arbitrary", "arbitrary")), name="diff_attention",
    )(q_t, k_r, v_t, row(lq1), row(lk1), row(lq2), row(lk2), row(subln_w))


def _merge_body(ha_ref, hb_ref, hc_ref, wa_ref, wb_ref, wc_ref, ga_ref, gb_ref, gc_ref, bias_ref, o_ref):
    def gated(h_ref, w_ref, g_ref, bi):
        y = jnp.dot(h_ref[...], w_ref[...], preferred_element_type=F32)
        return jax.nn.sigmoid(g_ref[...].astype(F32) + bias_ref[bi:bi + 1, :]) * y

    m = gated(ha_ref, wa_ref, ga_ref, 0) + gated(hb_ref, wb_ref, gb_ref, 1) + gated(hc_ref, wc_ref, gc_ref, 2)
    o_ref[...] = m.astype(o_ref.dtype)


def gated_merge(h_a, h_b, h_c, w_a, w_b, w_c, u, gate_b, layer, *, tm=1024, tn=512):
    s = h_a.shape[0]
    d = w_a.shape[2]
    tm = min(tm, s)
    wspec = lambda w: pl.BlockSpec((None, w.shape[1], tn), lambda i, j: (layer, 0, j))
    gblk = C_GATE // tn
    nblk = d // tn
    return pl.pallas_call(
        _merge_body,
        grid=(s // tm, d // tn),
        in_specs=[pl.BlockSpec((tm, h_a.shape[1]), lambda i, j: (i, 0)),
                  pl.BlockSpec((tm, h_b.shape[1]), lambda i, j: (i, 0)),
                  pl.BlockSpec((tm, h_c.shape[1]), lambda i, j: (i, 0)),
                  wspec(w_a), wspec(w_b), wspec(w_c),
                  pl.BlockSpec((tm, tn), lambda i, j: (i, gblk + j)),
                  pl.BlockSpec((tm, tn), lambda i, j: (i, gblk + nblk + j)),
                  pl.BlockSpec((tm, tn), lambda i, j: (i, gblk + 2 * nblk + j)),
                  pl.BlockSpec((3, tn), lambda i, j: (0, j))],
        out_specs=pl.BlockSpec((tm, tn), lambda i, j: (i, j)),
        out_shape=jax.ShapeDtypeStruct((s, d), BF16),
        compiler_params=_cparams(("arbitrary", "arbitrary")), name="gated_merge",
    )(h_a, h_b, h_c, w_a, w_b, w_c, u, u, u, gate_b)


def _residual_matmul_body(x_ref, a_ref, w_ref, o_ref):
    o_ref[...] = x_ref[...] + jnp.dot(a_ref[...], w_ref[...], preferred_element_type=F32)


def residual_matmul(x, a, w, layer, *, tm, tn, name):
    s, d = x.shape
    tm = min(tm, s)
    return pl.pallas_call(
        _residual_matmul_body,
        grid=(s // tm, d // tn),
        in_specs=[pl.BlockSpec((tm, tn), lambda i, j: (i, j)),
                  pl.BlockSpec((tm, a.shape[1]), lambda i, j: (i, 0)),
                  pl.BlockSpec((None, w.shape[1], tn), lambda i, j: (layer, 0, j))],
        out_specs=pl.BlockSpec((tm, tn), lambda i, j: (i, j)),
        out_shape=jax.ShapeDtypeStruct((s, d), F32),
        compiler_params=_cparams(("arbitrary", "arbitrary")), name=name,
    )(x, a, w)


def _rmsnorm_body(x_ref, w_ref, o_ref):
    x = x_ref[...]
    ms = jnp.mean(x * x, axis=-1, keepdims=True)
    o_ref[...] = x * lax.rsqrt(ms + RMS_EPS) * w_ref[...]


def rmsnorm(x, w, *, tm=512):
    s, d = x.shape
    tm = min(tm, s)
    return pl.pallas_call(
        _rmsnorm_body,
        grid=(s // tm,),
        in_specs=[pl.BlockSpec((tm, d), lambda i: (i, 0)), pl.BlockSpec((1, d), lambda i: (0, 0))],
        out_specs=pl.BlockSpec((tm, d), lambda i: (i, 0)),
        out_shape=jax.ShapeDtypeStruct((s, d), F32),
        compiler_params=_cparams(("arbitrary",)), name="final_rmsnorm",
    )(x, w.reshape(1, d))


FFN_HALO = 16


def _ffn_up_body(x_ref, xh_ref, nw_ref, wg_ref, wu_ref, cw_ref, o_ref, xn_ref, xnh_ref, ebuf, *, row_chunk):
    tm = x_ref.shape[0]
    i = pl.program_id(0)

    def normed(x):
        ms = jnp.mean(x * x, axis=-1, keepdims=True)
        return (x * lax.rsqrt(ms + RMS_EPS) * nw_ref[...]).astype(BF16)

    @pl.when(pl.program_id(1) == 0)
    def _():
        def chunk(c, carry):
            r0 = pl.multiple_of(c * row_chunk, row_chunk)
            xn_ref[pl.ds(r0, row_chunk), :] = normed(x_ref[pl.ds(r0, row_chunk), :])
            return carry
        lax.fori_loop(0, tm // row_chunk, chunk, 0)
        xnh_ref[...] = normed(xh_ref[...])

    gate_h = jnp.dot(xnh_ref[...], wg_ref[...], preferred_element_type=F32)
    ebuf[0:FFN_HALO, :] = jnp.where(i > 0, gate_h, 0.0)
    ebuf[FFN_HALO:, :] = jnp.dot(xn_ref[...], wg_ref[...], preferred_element_type=F32)
    up = jnp.dot(xn_ref[...], wu_ref[...], preferred_element_type=F32)
    base = FFN_HALO - (FFN_CONV_K - 1)
    acc = jnp.zeros(o_ref.shape, F32)
    for t in range(FFN_CONV_K):
        acc = acc + cw_ref[t:t + 1, :] * ebuf[base + t: base + t + tm, :]
    o_ref[...] = (_silu(acc) * up).astype(o_ref.dtype)


def ffn_up_act(x, nw, w_up, conv_w, layer, *, tm=1024, tn=512):
    s, d = x.shape
    tm = min(tm, s)
    nj = D_FF // tn
    hb = tm // FFN_HALO
    halo_idx = lambda i: jnp.maximum(i * hb - 1, 0)
    cw = jnp.zeros((8, D_FF), F32).at[:FFN_CONV_K].set(conv_w)
    return pl.pallas_call(
        functools.partial(_ffn_up_body, row_chunk=min(128, tm)),
        grid=(s // tm, nj),
        in_specs=[pl.BlockSpec((tm, d), lambda i, j: (i, 0)),
                  pl.BlockSpec((FFN_HALO, d), lambda i, j: (halo_idx(i), 0)),
                  pl.BlockSpec((1, d), lambda i, j: (0, 0)),
                  pl.BlockSpec((None, d, tn), lambda i, j: (layer, 0, j)),
                  pl.BlockSpec((None, d, tn), lambda i, j: (layer, 0, nj + j)),
                  pl.BlockSpec((8, tn), lambda i, j: (0, j))],
        out_specs=pl.BlockSpec((tm, tn), lambda i, j: (i, j)),
        out_shape=jax.ShapeDtypeStruct((s, D_FF), BF16),
        scratch_shapes=[pltpu.VMEM((tm, d), BF16), pltpu.VMEM((FFN_HALO, d), BF16),
                        pltpu.VMEM((FFN_HALO + tm, tn), F32)],
        compiler_params=_cparams(("arbitrary", "arbitrary")), name="ffn_up_act",
    )(x, x, nw.reshape(1, d), w_up, w_up, cw)


def kernel(x, positions, norm1_w, w_in, gate_b, conv_dw_w, conv_dw_b, conv_ln_w, conv_ln_b, conv_out_w,
           ssd_conv_w, ssd_conv_b, ssd_dt_bias, ssd_a_log, ssd_d, ssd_norm_w, ssd_out_w,
           da_lambda_q1, da_lambda_k1, da_lambda_q2, da_lambda_k2, da_subln_w, da_out_w,
           w_o, norm2_w, ffn_up_w, ffn_dw_w, ffn_down_w, final_norm_w):
    bsz, s_len, d = x.shape
    assert bsz == 1 and d == D_MODEL
    xc = x.reshape(s_len, d)
    pos = positions.reshape(s_len)
    w_in_bf = w_in.astype(BF16)
    w_tail = w_in_bf[:, :, DT_OFF + SSD_HEADS:]
    w_dt = jnp.zeros((DEPTH, d, DT_PAD), BF16).at[:, :, :SSD_HEADS].set(w_in_bf[:, :, DT_OFF:DT_OFF + SSD_HEADS])
    conv_out_bf, ssd_out_bf, da_out_bf = conv_out_w.astype(BF16), ssd_out_w.astype(BF16), da_out_w.astype(BF16)
    w_o_bf, ffn_up_bf, ffn_down_bf = w_o.astype(BF16), ffn_up_w.astype(BF16), ffn_down_w.astype(BF16)
    for l in range(DEPTH):
        lambda_init = 0.8 - 0.6 * math.exp(-0.3 * l)
        u, dt_raw = in_proj(xc, norm1_w[l], w_in_bf, w_tail, w_dt, l)
        h_a = conformer_pre(u, conv_dw_w[l], conv_dw_b[l], conv_ln_w[l], conv_ln_b[l])
        h_b = ssd_pre(u, dt_raw, ssd_conv_w[l], ssd_conv_b[l], ssd_dt_bias[l], ssd_a_log[l],
                      ssd_d[l], ssd_norm_w[l])
        q_t, k_r, v_t = rope_prep(u, pos)
        h_c = diff_attention(q_t, k_r, v_t, da_lambda_q1[l], da_lambda_k1[l], da_lambda_q2[l],
                             da_lambda_k2[l], da_subln_w[l], lambda_init)
        merged = gated_merge(h_a, h_b, h_c, conv_out_bf, ssd_out_bf, da_out_bf, u, gate_b[l], l)
        xc = residual_matmul(xc, merged, w_o_bf, l, tm=512, tn=d, name="wo_residual")
        act = ffn_up_act(xc, norm2_w[l], ffn_up_bf, ffn_dw_w[l], l)
        xc = residual_matmul(xc, act, ffn_down_bf, l, tm=1024, tn=256, name="ffn_down")
    return rmsnorm(xc, final_norm_w).reshape(bsz, s_len, d)
```

```python
import functools
import math

import jax
import jax.numpy as jnp
from jax import lax
from jax.experimental import pallas as pl
from jax.experimental.pallas import tpu as pltpu

F32 = jnp.float32
BF16 = jnp.bfloat16

D_MODEL = 2048
DEPTH = 2
CONV_CH = 1024
CONV_K = 31
SSD_HEADS = 32
SSD_HEADDIM = 64
SSD_INNER = SSD_HEADS * SSD_HEADDIM
SSD_GROUPS = 4
SSD_STATE = 128
SSD_CONV_K = 4
SSD_CHUNK = 128
SSD_GROUP_W = SSD_INNER // SSD_GROUPS
DA_HEADS = 8
DA_HEAD_DIM = 64
DA_V_DIM = 128
ROPE_DIM = 16
ROPE_THETA = 500000.0
D_FF = 5632
FFN_CONV_K = 3
RMS_EPS = 1e-6
LN_EPS = 1e-5
NEG_INF = -1e30

C_CONV_A = 0
C_CONV_G = 1024
C_Z = 2048
C_X = 4096
C_B = 6144
C_C = 6656
C_Q = 7168
C_K = 8192
C_V = 9216
C_GATE = 10240
U_WIDTH = 16384
DT_OFF = 7168
DT_PAD = 128

LANES = 128
VMEM_LIMIT = 56 * 1024 * 1024


def _cparams(sem, flags=None):
    return pltpu.CompilerParams(dimension_semantics=sem, vmem_limit_bytes=VMEM_LIMIT, flags=flags)


def _in_proj_body(x_ref, nw_ref, wh_ref, wt_ref, ws_ref, o_ref, os_ref, xn_ref, *, n_head_blocks, row_chunk):
    tm = x_ref.shape[0]
    j = pl.program_id(1)

    @pl.when(j == 0)
    def _():
        def chunk(c, carry):
            r0 = pl.multiple_of(c * row_chunk, row_chunk)
            x = x_ref[pl.ds(r0, row_chunk), :]
            ms = jnp.mean(x * x, axis=-1, keepdims=True)
            xn_ref[pl.ds(r0, row_chunk), :] = (x * lax.rsqrt(ms + RMS_EPS) * nw_ref[...]).astype(BF16)
            return carry
        lax.fori_loop(0, tm // row_chunk, chunk, 0)
        os_ref[...] = jnp.dot(xn_ref[...], ws_ref[...], preferred_element_type=F32)

    @pl.when(j < n_head_blocks)
    def _():
        o_ref[...] = jnp.dot(xn_ref[...], wh_ref[...], preferred_element_type=F32).astype(o_ref.dtype)

    @pl.when(j >= n_head_blocks)
    def _():
        o_ref[...] = jnp.dot(xn_ref[...], wt_ref[...], preferred_element_type=F32).astype(o_ref.dtype)


def in_proj(x, nw, w_all, w_tail, w_side, layer, *, tm=1024, tn=1024):
    s, d = x.shape
    tm = min(tm, s)
    nh, nt = DT_OFF // tn, w_tail.shape[2] // tn
    ns = w_side.shape[2]
    return pl.pallas_call(
        functools.partial(_in_proj_body, n_head_blocks=nh, row_chunk=min(128, tm)),
        grid=(s // tm, nh + nt),
        in_specs=[pl.BlockSpec((tm, d), lambda i, j: (i, 0)),
                  pl.BlockSpec((1, d), lambda i, j: (0, 0)),
                  pl.BlockSpec((None, d, tn), lambda i, j: (layer, 0, jnp.minimum(j, nh - 1))),
                  pl.BlockSpec((None, d, tn), lambda i, j: (layer, 0, jnp.maximum(j - nh, 0))),
                  pl.BlockSpec((None, d, ns), lambda i, j: (layer, 0, 0))],
        out_specs=[pl.BlockSpec((tm, tn), lambda i, j: (i, j)),
                   pl.BlockSpec((tm, ns), lambda i, j: (i, 0))],
        out_shape=[jax.ShapeDtypeStruct((s, (nh + nt) * tn), BF16), jax.ShapeDtypeStruct((s, ns), F32)],
        scratch_shapes=[pltpu.VMEM((tm, d), BF16)],
        compiler_params=_cparams(("arbitrary", "arbitrary")), name="in_proj",
    )(x, nw.reshape(1, d), w_all, w_tail, w_side)


CONV_HALO = 32
CONV_ROWS = 128


def _conformer_body(a_ref, g_ref, ah_ref, gh_ref, w_ref, b_ref, lnw_ref, lnb_ref, o_ref, hbuf, cbuf):
    ts = a_ref.shape[0]
    i = pl.program_id(0)
    halo = ah_ref[...].astype(F32) * jax.nn.sigmoid(gh_ref[...].astype(F32))
    hbuf[0:CONV_HALO, :] = jnp.where(i > 0, halo, 0.0)
    hbuf[CONV_HALO:, :] = a_ref[...].astype(F32) * jax.nn.sigmoid(g_ref[...].astype(F32))
    base = CONV_HALO - (CONV_K - 1)
    sub = 8
    rows = CONV_ROWS
    ext = rows + CONV_HALO
    for c in range(CONV_CH // LANES):
        cs = slice(c * LANES, (c + 1) * LANES)
        for r0 in range(0, ts, rows):
            slab = hbuf[r0:r0 + ext, cs]
            acc = jnp.zeros((rows, LANES), F32)
            for res in range(sub):
                taps = [k for k in range(CONV_K) if (base + k) % sub == res]
                if not taps:
                    continue
                rolled = slab if res == 0 else pltpu.roll(slab, ext - res, axis=0)
                for k in taps:
                    a = (base + k) // sub * sub
                    acc = acc + w_ref[k:k + 1, cs] * rolled[a:a + rows]
            cbuf[r0:r0 + rows, cs] = acc + b_ref[:, cs]
    y = cbuf[...]
    mu = jnp.mean(y, axis=-1, keepdims=True)
    yc = y - mu
    var = jnp.mean(yc * yc, axis=-1, keepdims=True)
    yn = yc * lax.rsqrt(var + LN_EPS) * lnw_ref[...] + lnb_ref[...]
    o_ref[...] = (yn * jax.nn.sigmoid(yn)).astype(o_ref.dtype)


def conformer_pre(u, w_dw, b_dw, ln_w, ln_b, *, ts=256):
    s = u.shape[0]
    ts = min(ts, s)
    hb = ts // CONV_HALO
    w_pad = jnp.zeros((32, CONV_CH), F32).at[:CONV_K].set(w_dw)
    row = lambda v: v.reshape(1, CONV_CH)
    halo_idx = lambda i: jnp.maximum(i * hb - 1, 0)
    return pl.pallas_call(
        _conformer_body,
        grid=(s // ts,),
        in_specs=[pl.BlockSpec((ts, CONV_CH), lambda i: (i, C_CONV_A // CONV_CH)),
                  pl.BlockSpec((ts, CONV_CH), lambda i: (i, C_CONV_G // CONV_CH)),
                  pl.BlockSpec((CONV_HALO, CONV_CH), lambda i: (halo_idx(i), C_CONV_A // CONV_CH)),
                  pl.BlockSpec((CONV_HALO, CONV_CH), lambda i: (halo_idx(i), C_CONV_G // CONV_CH)),
                  pl.BlockSpec((32, CONV_CH), lambda i: (0, 0)),
                  pl.BlockSpec((1, CONV_CH), lambda i: (0, 0)),
                  pl.BlockSpec((1, CONV_CH), lambda i: (0, 0)),
                  pl.BlockSpec((1, CONV_CH), lambda i: (0, 0))],
        out_specs=pl.BlockSpec((ts, CONV_CH), lambda i: (i, 0)),
        out_shape=jax.ShapeDtypeStruct((s, CONV_CH), BF16),
        scratch_shapes=[pltpu.VMEM((CONV_HALO + ts, CONV_CH), F32), pltpu.VMEM((ts, CONV_CH), F32)],
        compiler_params=_cparams(("arbitrary",)), name="conformer_pre",
    )(u, u, u, u, w_pad, row(b_dw), row(ln_w), row(ln_b))


SSD_HALO = 8
SSD_CONV_ROWS = 64
SSD_CONV_LANES = 256


def _split3(v):
    hi = v.astype(BF16)
    r1 = v - hi.astype(F32)
    mid = r1.astype(BF16)
    lo = (r1 - mid.astype(F32)).astype(BF16)
    return hi, mid, lo


def _dot01_left(m01, v):
    hi, mid, lo = _split3(v)
    d = lambda p: jnp.dot(m01, p, preferred_element_type=F32)
    return d(hi) + d(mid) + d(lo)


def _softplus(x):
    return jnp.maximum(x, 0.0) + jnp.log1p(jnp.exp(-jnp.abs(x)))


def _silu(x):
    return x * jax.nn.sigmoid(x)


def _ssd_body(x_ref, b_ref, c_ref, z_ref, xh_ref, bh_ref, ch_ref, dt_ref,
              cwx_ref, cwb_ref, cwc_ref, cbx_ref, cbb_ref, cbc_ref,
              dtb_ref, alog_ref, dx_ref, nw_ref, ex_ref,
              o_ref, state, ebuf, xs_c, b_c, c_c, ybuf, dt_s, cum_s, cx_s, dtx_s):
    ts = x_ref.shape[0]
    L = SSD_CHUNK
    i = pl.program_id(0)

    @pl.when(i == 0)
    def _():
        state[...] = jnp.zeros_like(state)

    def conv(in_ref, halo_ref, w_ref, bias_ref, out_ref):
        width = in_ref.shape[1]
        ebuf[0:SSD_HALO, 0:width] = jnp.where(i > 0, halo_ref[...].astype(F32), 0.0)
        ebuf[SSD_HALO:, 0:width] = in_ref[...].astype(F32)
        base = SSD_HALO - (SSD_CONV_K - 1)
        rb, cw = SSD_CONV_ROWS, SSD_CONV_LANES
        ext = rb + SSD_HALO

        def row_block(bi, carry):
            r0 = pl.multiple_of(bi * rb, rb)
            for c in range(width // cw):
                cs = slice(c * cw, (c + 1) * cw)
                slab = ebuf[pl.ds(r0, ext), cs]
                acc = bias_ref[:, cs]
                for k in range(SSD_CONV_K):
                    a, res = (base + k) // 8 * 8, (base + k) % 8
                    rolled = slab if res == 0 else pltpu.roll(slab, ext - res, axis=0)
                    acc = acc + w_ref[k:k + 1, cs] * rolled[a:a + rb]
                out_ref[pl.ds(r0, rb), cs] = _silu(acc)
            return carry

        lax.fori_loop(0, ts // rb, row_block, 0)

    conv(x_ref, xh_ref, cwx_ref, cbx_ref, xs_c)
    conv(b_ref, bh_ref, cwb_ref, cbb_ref, b_c)
    conv(c_ref, ch_ref, cwc_ref, cbc_ref, c_c)

    a_row = -jnp.exp(alog_ref[...])
    rr = lax.broadcasted_iota(jnp.int32, (L, L), 0)
    cc = lax.broadcasted_iota(jnp.int32, (L, L), 1)
    tri = rr >= cc
    tri01 = jnp.where(tri, 1.0, 0.0).astype(BF16)
    lane_lo = lax.broadcasted_iota(jnp.int32, (L, LANES), 1) < SSD_HEADDIM

    dt_all = _softplus(dt_ref[...] + dtb_ref[...])
    dt_s[...] = dt_all
    for ci in range(ts // L):
        cum_s[ci * L:(ci + 1) * L, :] = _dot01_left(tri01, dt_all[ci * L:(ci + 1) * L] * a_row)
    for src, dst in ((cum_s, cx_s), (dt_s, dtx_s)):
        dst[...] = jnp.dot(jnp.concatenate(_split3(src[...]), axis=1), ex_ref[...],
                           preferred_element_type=F32)

    def chunk(ci, carry):
        r0 = pl.multiple_of(ci * L, L)
        rows = pl.ds(r0, L)
        dt = dt_s[rows, :]
        cum = cum_s[rows, :]
        cum_t = cum.T
        dt_t = dt.T
        cum_x = cx_s[rows, :]
        dt_x = dtx_s[rows, :]
        cum_last = cum_x[L - 1:L, :]
        e_x = jnp.exp(cum_x)
        xs = xs_c[rows, :]
        xd = (xs * jnp.exp(cum_last - cum_x) * dt_x).astype(BF16)
        e_last = jnp.exp(cum_last)
        for g in range(SSD_GROUPS):
            gs = slice(g * SSD_GROUP_W, (g + 1) * SSD_GROUP_W)
            ns = slice(g * SSD_STATE, (g + 1) * SSD_STATE)
            bg = b_c[rows, ns]
            cg = c_c[rows, ns].astype(BF16)
            cb = lax.dot_general(cg, bg.astype(BF16), (((1,), (1,)), ((), ())),
                                 preferred_element_type=F32)
            st = state[:, gs]
            y_off = jnp.dot(cg, st.astype(BF16), preferred_element_type=F32) * e_x[:, gs]
            state[:, gs] = e_last[:, gs] * st + jnp.dot(bg.T.astype(BF16), xd[:, gs],
                                                        preferred_element_type=F32)
            for pr in range(SSD_GROUP_W // LANES):
                h0 = g * (SSD_HEADS // SSD_GROUPS) + 2 * pr
                ps = slice(h0 * SSD_HEADDIM, h0 * SSD_HEADDIM + LANES)
                x_pair = xs[:, ps].astype(BF16)
                ys = []
                for h in (h0, h0 + 1):
                    seg = cum[:, h:h + 1] - cum_t[h:h + 1, :]
                    decay = jnp.where(tri, jnp.exp(jnp.minimum(seg, 0.0)), 0.0)
                    wts = (cb * decay * dt_t[h:h + 1, :]).astype(BF16)
                    ys.append(jnp.dot(wts, x_pair, preferred_element_type=F32))
                y_pair = jnp.where(lane_lo, ys[0], ys[1])
                ybuf[:, ps] = y_pair + y_off[:, pr * LANES:(pr + 1) * LANES]
        y = ybuf[...] + xs * dx_ref[...]
        y = y * _silu(z_ref[rows, :].astype(F32))
        for g in range(SSD_GROUPS):
            gs = slice(g * SSD_GROUP_W, (g + 1) * SSD_GROUP_W)
            yg = y[:, gs]
            ms = jnp.mean(yg * yg, axis=-1, keepdims=True)
            o_ref[rows, gs] = (yg * lax.rsqrt(ms + RMS_EPS) * nw_ref[:, gs]).astype(o_ref.dtype)
        return carry

    lax.fori_loop(0, ts // L, chunk, 0)


def ssd_pre(u, dt_raw, conv_w, conv_b, dt_bias, a_log, d_skip, norm_w, *, ts=512):
    s = u.shape[0]
    ts = min(ts, s)
    hb = ts // SSD_HALO
    halo_idx = lambda i: jnp.maximum(i * hb - 1, 0)
    cw = jnp.zeros((8, conv_w.shape[1]), F32).at[:SSD_CONV_K].set(conv_w)
    n_bc = SSD_GROUPS * SSD_STATE
    cwx, cwb, cwc = cw[:, :SSD_INNER], cw[:, SSD_INNER:SSD_INNER + n_bc], cw[:, SSD_INNER + n_bc:]
    cb2 = conv_b.reshape(1, -1)
    cbx, cbb, cbc = cb2[:, :SSD_INNER], cb2[:, SSD_INNER:SSD_INNER + n_bc], cb2[:, SSD_INNER + n_bc:]
    pad_h = lambda v: jnp.zeros((1, DT_PAD), F32).at[0, :SSD_HEADS].set(v)
    dx = jnp.repeat(d_skip, SSD_HEADDIM).reshape(1, SSD_INNER)
    ex = (jnp.arange(SSD_INNER)[None, :] // SSD_HEADDIM == jnp.arange(DT_PAD)[:, None]).astype(BF16)
    ex = jnp.concatenate([ex, ex, ex], axis=0)
    const = lambda shape: pl.BlockSpec(shape, lambda i: (0, 0))
    return pl.pallas_call(
        _ssd_body,
        grid=(s // ts,),
        in_specs=[pl.BlockSpec((ts, SSD_INNER), lambda i: (i, C_X // SSD_INNER)),
                  pl.BlockSpec((ts, n_bc), lambda i: (i, C_B // n_bc)),
                  pl.BlockSpec((ts, n_bc), lambda i: (i, C_C // n_bc)),
                  pl.BlockSpec((ts, SSD_INNER), lambda i: (i, C_Z // SSD_INNER)),
                  pl.BlockSpec((SSD_HALO, SSD_INNER), lambda i: (halo_idx(i), C_X // SSD_INNER)),
                  pl.BlockSpec((SSD_HALO, n_bc), lambda i: (halo_idx(i), C_B // n_bc)),
                  pl.BlockSpec((SSD_HALO, n_bc), lambda i: (halo_idx(i), C_C // n_bc)),
                  pl.BlockSpec((ts, DT_PAD), lambda i: (i, 0)),
                  const((8, SSD_INNER)), const((8, n_bc)), const((8, n_bc)),
                  const((1, SSD_INNER)), const((1, n_bc)), const((1, n_bc)),
                  const((1, DT_PAD)), const((1, DT_PAD)), const((1, SSD_INNER)), const((1, SSD_INNER)),
                  const((3 * DT_PAD, SSD_INNER))],
        out_specs=pl.BlockSpec((ts, SSD_INNER), lambda i: (i, 0)),
        out_shape=jax.ShapeDtypeStruct((s, SSD_INNER), BF16),
        scratch_shapes=[pltpu.VMEM((SSD_STATE, SSD_INNER), F32),
                        pltpu.VMEM((SSD_HALO + ts, SSD_INNER), F32),
                        pltpu.VMEM((ts, SSD_INNER), F32),
                        pltpu.VMEM((ts, n_bc), F32),
                        pltpu.VMEM((ts, n_bc), F32),
                        pltpu.VMEM((SSD_CHUNK, SSD_INNER), F32),
                        pltpu.VMEM((ts, DT_PAD), F32), pltpu.VMEM((ts, DT_PAD), F32),
                        pltpu.VMEM((ts, SSD_INNER), F32), pltpu.VMEM((ts, SSD_INNER), F32)],
        compiler_params=_cparams(("arbitrary",)), name="ssd_pre",
    )(u, u, u, u, u, u, u, dt_raw, cwx, cwb, cwc, cbx, cbb, cbc,
      pad_h(dt_bias), pad_h(a_log), dx, norm_w.reshape(1, SSD_INNER), ex)


LOG2E = 1.4426950408889634
ATT_TQ = 1024
ATT_TK = ATT_TQ // 2
ATT_V_ROWS = DA_V_DIM + 16
ATT_NORM_SLACK = 1.02
ATT_MIN_ROW_SUM = 2.0 ** -64


def _rope_body(q_ref, k_ref, v_ref, pos_ref, inv_ref, qo_ref, ko_ref, vo_ref, kn_ref):
    ts = q_ref.shape[0]
    li = lax.broadcasted_iota(jnp.int32, (LANES, LANES), 0) // DA_HEAD_DIM
    lj = lax.broadcasted_iota(jnp.int32, (LANES, LANES), 1) // DA_HEAD_DIM
    same_comp = jnp.where(li == lj, 1.0, 0.0).astype(BF16)
    lane = lax.broadcasted_iota(jnp.int32, (ts, LANES), 1) % DA_HEAD_DIM
    half = ROPE_DIM // 2
    ang = pos_ref[...].astype(F32) * inv_ref[...]
    cos = jnp.cos(ang)
    sin = jnp.sin(ang)
    c_all = jnp.where(lane < ROPE_DIM, cos, 1.0)
    s_lo = jnp.where(lane < half, -sin, 0.0)
    s_hi = jnp.where((lane >= half) & (lane < ROPE_DIM), sin, 0.0)
    q_scale = DA_HEAD_DIM ** -0.5 * LOG2E

    def rotate(t):
        up = pltpu.roll(t, LANES - half, axis=1)
        dn = pltpu.roll(t, half, axis=1)
        return t * c_all + up * s_lo + dn * s_hi

    for c in range(q_ref.shape[1] // LANES):
        cs = slice(c * LANES, (c + 1) * LANES)
        kr = rotate(k_ref[:, cs].astype(F32)).astype(ko_ref.dtype)
        ko_ref[:, cs] = kr
        kf = kr.astype(F32)
        norm2 = jnp.dot((kf * kf).astype(BF16), same_comp, preferred_element_type=F32)
        kn_ref[c:c + 1, :] = jnp.sqrt(jnp.max(norm2, axis=0, keepdims=True)) * ATT_NORM_SLACK
        qo_ref[cs, :] = (rotate(q_ref[:, cs].astype(F32)) * q_scale).T.astype(qo_ref.dtype)
        vo_ref[c, 0:DA_V_DIM, :] = v_ref[:, cs].astype(F32).T.astype(vo_ref.dtype)
        extra = lax.broadcasted_iota(jnp.int32, (ATT_V_ROWS - DA_V_DIM, ts), 0) == 0
        vo_ref[c, DA_V_DIM:ATT_V_ROWS, :] = jnp.where(extra, 1.0, 0.0).astype(vo_ref.dtype)


def rope_prep(u, positions, *, ts=ATT_TK):
    s = u.shape[0]
    ts = min(ts, s)
    w = DA_HEADS * 2 * DA_HEAD_DIM
    inv = 1.0 / (ROPE_THETA ** (jnp.arange(0, ROPE_DIM, 2, dtype=F32) / ROPE_DIM))
    lane = jnp.arange(LANES) % DA_HEAD_DIM
    inv_lane = jnp.where(lane < ROPE_DIM, inv[lane % (ROPE_DIM // 2)], 0.0).reshape(1, LANES)
    return pl.pallas_call(
        _rope_body,
        grid=(s // ts,),
        in_specs=[pl.BlockSpec((ts, w), lambda i: (i, C_Q // w)),
                  pl.BlockSpec((ts, w), lambda i: (i, C_K // w)),
                  pl.BlockSpec((ts, w), lambda i: (i, C_V // w)),
                  pl.BlockSpec((ts, 1), lambda i: (i, 0)),
                  pl.BlockSpec((1, LANES), lambda i: (0, 0))],
        out_specs=[pl.BlockSpec((w, ts), lambda i: (0, i)), pl.BlockSpec((ts, w), lambda i: (i, 0)),
                   pl.BlockSpec((DA_HEADS, None, ATT_V_ROWS, ts), lambda i: (0, i, 0, 0)),
                   pl.BlockSpec((None, DA_HEADS, LANES), lambda i: (i, 0, 0))],
        out_shape=[jax.ShapeDtypeStruct((w, s), BF16), jax.ShapeDtypeStruct((s, w), BF16),
                   jax.ShapeDtypeStruct((DA_HEADS, s // ts, ATT_V_ROWS, ts), BF16),
                   jax.ShapeDtypeStruct((s // ts, DA_HEADS, LANES), F32)],
        compiler_params=_cparams(("arbitrary",)), name="rope_prep",
    )(u, u, u, positions.reshape(s, 1), inv_lane)


def _attn_body(qt_ref, k_ref, vt_ref, kn_ref, lq1_ref, lk1_ref, lq2_ref, lk2_ref, sw_ref, o_ref,
               m_ref, acc_ref, sa_ref, sb_ref, ma_ref, mb_ref, *, lambda_init):
    tq = qt_ref.shape[1]
    tk = vt_ref.shape[2]
    assert tq == 2 * tk
    i = pl.program_id(1)
    row = lax.broadcasted_iota(jnp.int32, (LANES, tq), 0)
    qt = qt_ref[...]
    zero = jnp.zeros_like(qt)
    qc = (jnp.where(row < DA_HEAD_DIM, qt, zero), jnp.where(row >= DA_HEAD_DIM, qt, zero))
    m_ref[...] = jnp.full(m_ref.shape, NEG_INF, F32)
    acc_ref[...] = jnp.zeros(acc_ref.shape, F32)

    def scores(j, s_ref, mc_ref):
        kb = k_ref[pl.ds(pl.multiple_of(j * tk, tk), tk), :]
        for c in range(2):
            s = jnp.dot(kb, qc[c], preferred_element_type=F32)
            s_ref[c] = s
            mc_ref[c] = jnp.max(s, axis=0, keepdims=True)

    def consume(j, s_ref, mc_ref, masked):
        vtb = vt_ref[j]
        for c in range(2):
            s = s_ref[c]
            if masked:
                kk = lax.broadcasted_iota(jnp.int32, (tk, tq), 0) + (j * tk - i * tq)
                qq = lax.broadcasted_iota(jnp.int32, (tk, tq), 1)
                s = jnp.where(kk <= qq, s, NEG_INF)
                m_cur = jnp.max(s, axis=0, keepdims=True)
            else:
                m_cur = mc_ref[c]
            m_prev = m_ref[c]
            m_new = jnp.maximum(m_prev, m_cur)
            alpha = jnp.exp2(m_prev - m_new)
            p = jnp.exp2(s - m_new).astype(BF16)
            acc_ref[c] = alpha * acc_ref[c] + jnp.dot(vtb, p, preferred_element_type=F32)
            m_ref[c] = m_new

    def pair(t, carry):
        j0 = 2 * t
        scores(j0 + 1, sb_ref, mb_ref)
        consume(j0, sa_ref, ma_ref, False)
        scores(j0 + 2, sa_ref, ma_ref)
        consume(j0 + 1, sb_ref, mb_ref, False)
        return carry

    qn = [jnp.sqrt(jnp.sum(jnp.square(qc[c].astype(F32)), axis=0, keepdims=True)) for c in range(2)]

    def stream(j, masked):
        kb = k_ref[pl.ds(pl.multiple_of(j * tk, tk), tk), :]
        vtb = vt_ref[j]
        kn = kn_ref[pl.ds(j, 1), :]
        for c in range(2):
            s = jnp.dot(kb, qc[c], preferred_element_type=F32)
            if masked:
                kk = lax.broadcasted_iota(jnp.int32, (tk, tq), 0) + (j * tk - i * tq)
                qq = lax.broadcasted_iota(jnp.int32, (tk, tq), 1)
                s = jnp.where(kk <= qq, s, NEG_INF)
            m_prev = m_ref[c]
            m_new = jnp.maximum(m_prev, qn[c] * kn[:, c * DA_HEAD_DIM:c * DA_HEAD_DIM + 1])
            alpha = jnp.exp2(m_prev - m_new)
            p = jnp.exp2(s - m_new).astype(BF16)
            acc_ref[c] = alpha * acc_ref[c] + jnp.dot(vtb, p, preferred_element_type=F32)
            m_ref[c] = m_new

    def stream_pair(t, carry):
        stream(2 * t, False)
        stream(2 * t + 1, False)
        return carry

    lax.fori_loop(0, i, stream_pair, 0)
    stream(2 * i, True)
    stream(2 * i + 1, True)

    row_sums = jnp.minimum(acc_ref[0, DA_V_DIM:DA_V_DIM + 1, :], acc_ref[1, DA_V_DIM:DA_V_DIM + 1, :])
    healthy = jnp.min(row_sums) >= ATT_MIN_ROW_SUM

    @pl.when(jnp.logical_not(healthy))
    def _():
        m_ref[...] = jnp.full(m_ref.shape, NEG_INF, F32)
        acc_ref[...] = jnp.zeros(acc_ref.shape, F32)
        scores(0, sa_ref, ma_ref)
        lax.fori_loop(0, i, pair, 0)
        scores(2 * i + 1, sb_ref, mb_ref)
        consume(2 * i, sa_ref, ma_ref, True)
        consume(2 * i + 1, sb_ref, mb_ref, True)

    lam = (jnp.exp(jnp.sum(lq1_ref[...] * lk1_ref[...], axis=-1, keepdims=True))
           - jnp.exp(jnp.sum(lq2_ref[...] * lk2_ref[...], axis=-1, keepdims=True)) + lambda_init)
    nd = DA_V_DIM
    ot = (acc_ref[0, 0:nd, :] / acc_ref[0, nd:nd + 1, :]
          - lam * (acc_ref[1, 0:nd, :] / acc_ref[1, nd:nd + 1, :]))
    ms = jnp.mean(ot * ot, axis=0, keepdims=True)
    ot = ot * (lax.rsqrt(ms + LN_EPS) * (1.0 - lambda_init))
    o_ref[...] = (ot.T * sw_ref[...]).astype(o_ref.dtype)


def diff_attention(q_t, k_r, v_t, k_norm, lq1, lk1, lq2, lk2, subln_w, lambda_init, *, tq=ATT_TQ):
    s = k_r.shape[0]
    nkb, tk = v_t.shape[1], v_t.shape[3]
    tq = min(tq, s)
    row = lambda v: v.reshape(1, -1)
    const = lambda n: pl.BlockSpec((1, n), lambda h, i: (0, 0))
    return pl.pallas_call(
        functools.partial(_attn_body, lambda_init=lambda_init),
        grid=(DA_HEADS, s // tq),
        in_specs=[pl.BlockSpec((LANES, tq), lambda h, i: (h, i)),
                  pl.BlockSpec((s, LANES), lambda h, i: (0, h)),
                  pl.BlockSpec((None, nkb, ATT_V_ROWS, tk), lambda h, i: (h, 0, 0, 0)),
                  pl.BlockSpec((None, nkb, LANES), lambda h, i: (h, 0, 0)),
                  const(DA_HEAD_DIM), const(DA_HEAD_DIM), const(DA_HEAD_DIM), const(DA_HEAD_DIM),
                  const(DA_V_DIM)],
        out_specs=pl.BlockSpec((tq, LANES), lambda h, i: (i, h)),
        out_shape=jax.ShapeDtypeStruct((s, DA_HEADS * DA_V_DIM), BF16),
        scratch_shapes=[pltpu.VMEM((2, 1, tq), F32),
                        pltpu.VMEM((2, ATT_V_ROWS, tq), F32),
                        pltpu.VMEM((2, tk, tq), F32), pltpu.VMEM((2, tk, tq), F32),
                        pltpu.VMEM((2, 1, tq), F32), pltpu.VMEM((2, 1, tq), F32)],
        compiler_params=_cparams(("arbitrary", "arbitrary")), name="diff_attention",
    )(q_t, k_r, v_t, jnp.swapaxes(k_norm, 0, 1), row(lq1), row(lk1), row(lq2), row(lk2), row(subln_w))


def _merge_body(ha_ref, hb_ref, hc_ref, wa_ref, wb_ref, wc_ref, ga_ref, gb_ref, gc_ref, bias_ref, o_ref):
    def gated(h_ref, w_ref, g_ref, bi):
        y = jnp.dot(h_ref[...], w_ref[...], preferred_element_type=F32)
        return jax.nn.sigmoid(g_ref[...].astype(F32) + bias_ref[bi:bi + 1, :]) * y

    m = gated(ha_ref, wa_ref, ga_ref, 0) + gated(hb_ref, wb_ref, gb_ref, 1) + gated(hc_ref, wc_ref, gc_ref, 2)
    o_ref[...] = m.astype(o_ref.dtype)


def gated_merge(h_a, h_b, h_c, w_a, w_b, w_c, u, gate_b, layer, *, tm=1024, tn=512):
    s = h_a.shape[0]
    d = w_a.shape[2]
    tm = min(tm, s)
    wspec = lambda w: pl.BlockSpec((None, w.shape[1], tn), lambda i, j: (layer, 0, j))
    gblk = C_GATE // tn
    nblk = d // tn
    return pl.pallas_call(
        _merge_body,
        grid=(s // tm, d // tn),
        in_specs=[pl.BlockSpec((tm, h_a.shape[1]), lambda i, j: (i, 0)),
                  pl.BlockSpec((tm, h_b.shape[1]), lambda i, j: (i, 0)),
                  pl.BlockSpec((tm, h_c.shape[1]), lambda i, j: (i, 0)),
                  wspec(w_a), wspec(w_b), wspec(w_c),
                  pl.BlockSpec((tm, tn), lambda i, j: (i, gblk + j)),
                  pl.BlockSpec((tm, tn), lambda i, j: (i, gblk + nblk + j)),
                  pl.BlockSpec((tm, tn), lambda i, j: (i, gblk + 2 * nblk + j)),
                  pl.BlockSpec((3, tn), lambda i, j: (0, j))],
        out_specs=pl.BlockSpec((tm, tn), lambda i, j: (i, j)),
        out_shape=jax.ShapeDtypeStruct((s, d), BF16),
        compiler_params=_cparams(("arbitrary", "arbitrary")), name="gated_merge",
    )(h_a, h_b, h_c, w_a, w_b, w_c, u, u, u, gate_b)


def _residual_matmul_body(x_ref, a_ref, w_ref, o_ref):
    o_ref[...] = x_ref[...] + jnp.dot(a_ref[...], w_ref[...], preferred_element_type=F32)


def residual_matmul(x, a, w, layer, *, tm, tn, name):
    s, d = x.shape
    tm = min(tm, s)
    return pl.pallas_call(
        _residual_matmul_body,
        grid=(s // tm, d // tn),
        in_specs=[pl.BlockSpec((tm, tn), lambda i, j: (i, j)),
                  pl.BlockSpec((tm, a.shape[1]), lambda i, j: (i, 0)),
                  pl.BlockSpec((None, w.shape[1], tn), lambda i, j: (layer, 0, j))],
        out_specs=pl.BlockSpec((tm, tn), lambda i, j: (i, j)),
        out_shape=jax.ShapeDtypeStruct((s, d), F32),
        compiler_params=_cparams(("arbitrary", "arbitrary")), name=name,
    )(x, a, w)


def _rmsnorm_body(x_ref, w_ref, o_ref):
    x = x_ref[...]
    ms = jnp.mean(x * x, axis=-1, keepdims=True)
    o_ref[...] = x * lax.rsqrt(ms + RMS_EPS) * w_ref[...]


def rmsnorm(x, w, *, tm=512):
    s, d = x.shape
    tm = min(tm, s)
    return pl.pallas_call(
        _rmsnorm_body,
        grid=(s // tm,),
        in_specs=[pl.BlockSpec((tm, d), lambda i: (i, 0)), pl.BlockSpec((1, d), lambda i: (0, 0))],
        out_specs=pl.BlockSpec((tm, d), lambda i: (i, 0)),
        out_shape=jax.ShapeDtypeStruct((s, d), F32),
        compiler_params=_cparams(("arbitrary",)), name="final_rmsnorm",
    )(x, w.reshape(1, d))


FFN_HALO = 16


def _ffn_up_body(x_ref, xh_ref, nw_ref, wg_ref, wu_ref, cw_ref, o_ref, xn_ref, xnh_ref, ebuf, *, row_chunk):
    tm = x_ref.shape[0]
    i = pl.program_id(0)

    def normed(x):
        ms = jnp.mean(x * x, axis=-1, keepdims=True)
        return (x * lax.rsqrt(ms + RMS_EPS) * nw_ref[...]).astype(BF16)

    @pl.when(pl.program_id(1) == 0)
    def _():
        def chunk(c, carry):
            r0 = pl.multiple_of(c * row_chunk, row_chunk)
            xn_ref[pl.ds(r0, row_chunk), :] = normed(x_ref[pl.ds(r0, row_chunk), :])
            return carry
        lax.fori_loop(0, tm // row_chunk, chunk, 0)
        xnh_ref[...] = normed(xh_ref[...])

    gate_h = jnp.dot(xnh_ref[...], wg_ref[...], preferred_element_type=F32)
    ebuf[0:FFN_HALO, :] = jnp.where(i > 0, gate_h, 0.0)
    ebuf[FFN_HALO:, :] = jnp.dot(xn_ref[...], wg_ref[...], preferred_element_type=F32)
    up = jnp.dot(xn_ref[...], wu_ref[...], preferred_element_type=F32)
    base = FFN_HALO - (FFN_CONV_K - 1)
    acc = jnp.zeros(o_ref.shape, F32)
    for t in range(FFN_CONV_K):
        acc = acc + cw_ref[t:t + 1, :] * ebuf[base + t: base + t + tm, :]
    o_ref[...] = (_silu(acc) * up).astype(o_ref.dtype)


def ffn_up_act(x, nw, w_up, conv_w, layer, *, tm=1024, tn=512):
    s, d = x.shape
    tm = min(tm, s)
    nj = D_FF // tn
    hb = tm // FFN_HALO
    halo_idx = lambda i: jnp.maximum(i * hb - 1, 0)
    cw = jnp.zeros((8, D_FF), F32).at[:FFN_CONV_K].set(conv_w)
    return pl.pallas_call(
        functools.partial(_ffn_up_body, row_chunk=min(128, tm)),
        grid=(s // tm, nj),
        in_specs=[pl.BlockSpec((tm, d), lambda i, j: (i, 0)),
                  pl.BlockSpec((FFN_HALO, d), lambda i, j: (halo_idx(i), 0)),
                  pl.BlockSpec((1, d), lambda i, j: (0, 0)),
                  pl.BlockSpec((None, d, tn), lambda i, j: (layer, 0, j)),
                  pl.BlockSpec((None, d, tn), lambda i, j: (layer, 0, nj + j)),
                  pl.BlockSpec((8, tn), lambda i, j: (0, j))],
        out_specs=pl.BlockSpec((tm, tn), lambda i, j: (i, j)),
        out_shape=jax.ShapeDtypeStruct((s, D_FF), BF16),
        scratch_shapes=[pltpu.VMEM((tm, d), BF16), pltpu.VMEM((FFN_HALO, d), BF16),
                        pltpu.VMEM((FFN_HALO + tm, tn), F32)],
        compiler_params=_cparams(("arbitrary", "arbitrary")), name="ffn_up_act",
    )(x, x, nw.reshape(1, d), w_up, w_up, cw)


def kernel(x, positions, norm1_w, w_in, gate_b, conv_dw_w, conv_dw_b, conv_ln_w, conv_ln_b, conv_out_w,
           ssd_conv_w, ssd_conv_b, ssd_dt_bias, ssd_a_log, ssd_d, ssd_norm_w, ssd_out_w,
           da_lambda_q1, da_lambda_k1, da_lambda_q2, da_lambda_k2, da_subln_w, da_out_w,
           w_o, norm2_w, ffn_up_w, ffn_dw_w, ffn_down_w, final_norm_w):
    bsz, s_len, d = x.shape
    assert bsz == 1 and d == D_MODEL
    xc = x.reshape(s_len, d)
    pos = positions.reshape(s_len)
    w_in_bf = w_in.astype(BF16)
    w_tail = w_in_bf[:, :, DT_OFF + SSD_HEADS:]
    w_dt = jnp.zeros((DEPTH, d, DT_PAD), BF16).at[:, :, :SSD_HEADS].set(w_in_bf[:, :, DT_OFF:DT_OFF + SSD_HEADS])
    conv_out_bf, ssd_out_bf, da_out_bf = conv_out_w.astype(BF16), ssd_out_w.astype(BF16), da_out_w.astype(BF16)
    w_o_bf, ffn_up_bf, ffn_down_bf = w_o.astype(BF16), ffn_up_w.astype(BF16), ffn_down_w.astype(BF16)
    for l in range(DEPTH):
        lambda_init = 0.8 - 0.6 * math.exp(-0.3 * l)
        u, dt_raw = in_proj(xc, norm1_w[l], w_in_bf, w_tail, w_dt, l)
        h_a = conformer_pre(u, conv_dw_w[l], conv_dw_b[l], conv_ln_w[l], conv_ln_b[l])
        h_b = ssd_pre(u, dt_raw, ssd_conv_w[l], ssd_conv_b[l], ssd_dt_bias[l], ssd_a_log[l],
                      ssd_d[l], ssd_norm_w[l])
        q_t, k_r, v_t, k_norm = rope_prep(u, pos)
        h_c = diff_attention(q_t, k_r, v_t, k_norm, da_lambda_q1[l], da_lambda_k1[l], da_lambda_q2[l],
                             da_lambda_k2[l], da_subln_w[l], lambda_init)
        merged = gated_merge(h_a, h_b, h_c, conv_out_bf, ssd_out_bf, da_out_bf, u, gate_b[l], l)
        xc = residual_matmul(xc, merged, w_o_bf, l, tm=512, tn=d, name="wo_residual")
        act = ffn_up_act(xc, norm2_w[l], ffn_up_bf, ffn_dw_w[l], l)
        xc = residual_matmul(xc, act, ffn_down_bf, l, tm=1024, tn=256, name="ffn_down")
    return rmsnorm(xc, final_norm_w).reshape(bsz, s_len, d)
```

```python
import functools
import math

import jax
import jax.numpy as jnp
from jax import lax
from jax.experimental import pallas as pl
from jax.experimental.pallas import tpu as pltpu

F32 = jnp.float32
BF16 = jnp.bfloat16

D_MODEL = 2048
DEPTH = 2
CONV_CH = 1024
CONV_K = 31
SSD_HEADS = 32
SSD_HEADDIM = 64
SSD_INNER = SSD_HEADS * SSD_HEADDIM
SSD_GROUPS = 4
SSD_STATE = 128
SSD_CONV_K = 4
SSD_CHUNK = 128
SSD_GROUP_W = SSD_INNER // SSD_GROUPS
DA_HEADS = 8
DA_HEAD_DIM = 64
DA_V_DIM = 128
ROPE_DIM = 16
ROPE_THETA = 500000.0
D_FF = 5632
FFN_CONV_K = 3
RMS_EPS = 1e-6
LN_EPS = 1e-5
NEG_INF = -1e30

C_CONV_A = 0
C_CONV_G = 1024
C_Z = 2048
C_X = 4096
C_B = 6144
C_C = 6656
C_Q = 7168
C_K = 8192
C_V = 9216
C_GATE = 10240
U_WIDTH = 16384
DT_OFF = 7168
DT_PAD = 128

LANES = 128
VMEM_LIMIT = 56 * 1024 * 1024


def _cparams(sem, flags=None):
    return pltpu.CompilerParams(dimension_semantics=sem, vmem_limit_bytes=VMEM_LIMIT, flags=flags)


def _in_proj_body(x_ref, nw_ref, wh_ref, wt_ref, ws_ref, o_ref, os_ref, xn_ref, *, n_head_blocks, row_chunk):
    tm = x_ref.shape[0]
    j = pl.program_id(1)

    @pl.when(j == 0)
    def _():
        def chunk(c, carry):
            r0 = pl.multiple_of(c * row_chunk, row_chunk)
            x = x_ref[pl.ds(r0, row_chunk), :]
            ms = jnp.mean(x * x, axis=-1, keepdims=True)
            xn_ref[pl.ds(r0, row_chunk), :] = (x * lax.rsqrt(ms + RMS_EPS) * nw_ref[...]).astype(BF16)
            return carry
        lax.fori_loop(0, tm // row_chunk, chunk, 0)
        os_ref[...] = jnp.dot(xn_ref[...], ws_ref[...], preferred_element_type=F32)

    @pl.when(j < n_head_blocks)
    def _():
        o_ref[...] = jnp.dot(xn_ref[...], wh_ref[...], preferred_element_type=F32).astype(o_ref.dtype)

    @pl.when(j >= n_head_blocks)
    def _():
        o_ref[...] = jnp.dot(xn_ref[...], wt_ref[...], preferred_element_type=F32).astype(o_ref.dtype)


def in_proj(x, nw, w_all, w_tail, w_side, layer, *, tm=1024, tn=1024):
    s, d = x.shape
    tm = min(tm, s)
    nh, nt = DT_OFF // tn, w_tail.shape[2] // tn
    ns = w_side.shape[2]
    return pl.pallas_call(
        functools.partial(_in_proj_body, n_head_blocks=nh, row_chunk=min(128, tm)),
        grid=(s // tm, nh + nt),
        in_specs=[pl.BlockSpec((tm, d), lambda i, j: (i, 0)),
                  pl.BlockSpec((1, d), lambda i, j: (0, 0)),
                  pl.BlockSpec((None, d, tn), lambda i, j: (layer, 0, jnp.minimum(j, nh - 1))),
                  pl.BlockSpec((None, d, tn), lambda i, j: (layer, 0, jnp.maximum(j - nh, 0))),
                  pl.BlockSpec((None, d, ns), lambda i, j: (layer, 0, 0))],
        out_specs=[pl.BlockSpec((tm, tn), lambda i, j: (i, j)),
                   pl.BlockSpec((tm, ns), lambda i, j: (i, 0))],
        out_shape=[jax.ShapeDtypeStruct((s, (nh + nt) * tn), BF16), jax.ShapeDtypeStruct((s, ns), F32)],
        scratch_shapes=[pltpu.VMEM((tm, d), BF16)],
        compiler_params=_cparams(("arbitrary", "arbitrary")), name="in_proj",
    )(x, nw.reshape(1, d), w_all, w_tail, w_side)


CONV_HALO = 32
CONV_ROWS = 128


def _conformer_body(a_ref, g_ref, ah_ref, gh_ref, w_ref, b_ref, lnw_ref, lnb_ref, o_ref, hbuf, cbuf):
    ts = a_ref.shape[0]
    i = pl.program_id(0)
    halo = ah_ref[...].astype(F32) * jax.nn.sigmoid(gh_ref[...].astype(F32))
    hbuf[0:CONV_HALO, :] = jnp.where(i > 0, halo, 0.0)
    hbuf[CONV_HALO:, :] = a_ref[...].astype(F32) * jax.nn.sigmoid(g_ref[...].astype(F32))
    base = CONV_HALO - (CONV_K - 1)
    sub = 8
    rows = CONV_ROWS
    ext = rows + CONV_HALO
    for c in range(CONV_CH // LANES):
        cs = slice(c * LANES, (c + 1) * LANES)
        for r0 in range(0, ts, rows):
            slab = hbuf[r0:r0 + ext, cs]
            acc = jnp.zeros((rows, LANES), F32)
            for res in range(sub):
                taps = [k for k in range(CONV_K) if (base + k) % sub == res]
                if not taps:
                    continue
                rolled = slab if res == 0 else pltpu.roll(slab, ext - res, axis=0)
                for k in taps:
                    a = (base + k) // sub * sub
                    acc = acc + w_ref[k:k + 1, cs] * rolled[a:a + rows]
            cbuf[r0:r0 + rows, cs] = acc + b_ref[:, cs]
    y = cbuf[...]
    mu = jnp.mean(y, axis=-1, keepdims=True)
    yc = y - mu
    var = jnp.mean(yc * yc, axis=-1, keepdims=True)
    yn = yc * lax.rsqrt(var + LN_EPS) * lnw_ref[...] + lnb_ref[...]
    o_ref[...] = (yn * jax.nn.sigmoid(yn)).astype(o_ref.dtype)


def conformer_pre(u, w_dw, b_dw, ln_w, ln_b, *, ts=256):
    s = u.shape[0]
    ts = min(ts, s)
    hb = ts // CONV_HALO
    w_pad = jnp.zeros((32, CONV_CH), F32).at[:CONV_K].set(w_dw)
    row = lambda v: v.reshape(1, CONV_CH)
    halo_idx = lambda i: jnp.maximum(i * hb - 1, 0)
    return pl.pallas_call(
        _conformer_body,
        grid=(s // ts,),
        in_specs=[pl.BlockSpec((ts, CONV_CH), lambda i: (i, C_CONV_A // CONV_CH)),
                  pl.BlockSpec((ts, CONV_CH), lambda i: (i, C_CONV_G // CONV_CH)),
                  pl.BlockSpec((CONV_HALO, CONV_CH), lambda i: (halo_idx(i), C_CONV_A // CONV_CH)),
                  pl.BlockSpec((CONV_HALO, CONV_CH), lambda i: (halo_idx(i), C_CONV_G // CONV_CH)),
                  pl.BlockSpec((32, CONV_CH), lambda i: (0, 0)),
                  pl.BlockSpec((1, CONV_CH), lambda i: (0, 0)),
                  pl.BlockSpec((1, CONV_CH), lambda i: (0, 0)),
                  pl.BlockSpec((1, CONV_CH), lambda i: (0, 0))],
        out_specs=pl.BlockSpec((ts, CONV_CH), lambda i: (i, 0)),
        out_shape=jax.ShapeDtypeStruct((s, CONV_CH), BF16),
        scratch_shapes=[pltpu.VMEM((CONV_HALO + ts, CONV_CH), F32), pltpu.VMEM((ts, CONV_CH), F32)],
        compiler_params=_cparams(("arbitrary",)), name="conformer_pre",
    )(u, u, u, u, w_pad, row(b_dw), row(ln_w), row(ln_b))


SSD_HALO = 8
SSD_CONV_ROWS = 64
SSD_CONV_LANES = 256


def _split3(v):
    hi = v.astype(BF16)
    r1 = v - hi.astype(F32)
    mid = r1.astype(BF16)
    lo = (r1 - mid.astype(F32)).astype(BF16)
    return hi, mid, lo


def _dot01_left(m01, v):
    hi, mid, lo = _split3(v)
    d = lambda p: jnp.dot(m01, p, preferred_element_type=F32)
    return d(hi) + d(mid) + d(lo)


def _softplus(x):
    return jnp.maximum(x, 0.0) + jnp.log1p(jnp.exp(-jnp.abs(x)))


def _silu(x):
    return x * jax.nn.sigmoid(x)


def _ssd_body(x_ref, b_ref, c_ref, z_ref, xh_ref, bh_ref, ch_ref, dt_ref,
              cwx_ref, cwb_ref, cwc_ref, cbx_ref, cbb_ref, cbc_ref,
              dtb_ref, alog_ref, dx_ref, nw_ref, ex_ref,
              o_ref, state, ebuf, xs_c, b_c, c_c, ybuf, dt_s, cum_s, cx_s, dtx_s):
    ts = x_ref.shape[0]
    L = SSD_CHUNK
    i = pl.program_id(0)

    @pl.when(i == 0)
    def _():
        state[...] = jnp.zeros_like(state)

    def conv(in_ref, halo_ref, w_ref, bias_ref, out_ref):
        width = in_ref.shape[1]
        ebuf[0:SSD_HALO, 0:width] = jnp.where(i > 0, halo_ref[...].astype(F32), 0.0)
        ebuf[SSD_HALO:, 0:width] = in_ref[...].astype(F32)
        base = SSD_HALO - (SSD_CONV_K - 1)
        rb, cw = SSD_CONV_ROWS, SSD_CONV_LANES
        ext = rb + SSD_HALO

        def row_block(bi, carry):
            r0 = pl.multiple_of(bi * rb, rb)
            for c in range(width // cw):
                cs = slice(c * cw, (c + 1) * cw)
                slab = ebuf[pl.ds(r0, ext), cs]
                acc = bias_ref[:, cs]
                for k in range(SSD_CONV_K):
                    a, res = (base + k) // 8 * 8, (base + k) % 8
                    rolled = slab if res == 0 else pltpu.roll(slab, ext - res, axis=0)
                    acc = acc + w_ref[k:k + 1, cs] * rolled[a:a + rb]
                out_ref[pl.ds(r0, rb), cs] = _silu(acc)
            return carry

        lax.fori_loop(0, ts // rb, row_block, 0)

    conv(x_ref, xh_ref, cwx_ref, cbx_ref, xs_c)
    conv(b_ref, bh_ref, cwb_ref, cbb_ref, b_c)
    conv(c_ref, ch_ref, cwc_ref, cbc_ref, c_c)

    a_row = -jnp.exp(alog_ref[...])
    rr = lax.broadcasted_iota(jnp.int32, (L, L), 0)
    cc = lax.broadcasted_iota(jnp.int32, (L, L), 1)
    tri = rr >= cc
    tri01 = jnp.where(tri, 1.0, 0.0).astype(BF16)
    lane_lo = lax.broadcasted_iota(jnp.int32, (L, LANES), 1) < SSD_HEADDIM

    dt_all = _softplus(dt_ref[...] + dtb_ref[...])
    dt_s[...] = dt_all
    for ci in range(ts // L):
        cum_s[ci * L:(ci + 1) * L, :] = _dot01_left(tri01, dt_all[ci * L:(ci + 1) * L] * a_row)
    for src, dst in ((cum_s, cx_s), (dt_s, dtx_s)):
        dst[...] = jnp.dot(jnp.concatenate(_split3(src[...]), axis=1), ex_ref[...],
                           preferred_element_type=F32)

    def chunk(ci, carry):
        r0 = pl.multiple_of(ci * L, L)
        rows = pl.ds(r0, L)
        dt = dt_s[rows, :]
        cum = cum_s[rows, :]
        cum_t = cum.T
        dt_t = dt.T
        cum_x = cx_s[rows, :]
        dt_x = dtx_s[rows, :]
        cum_last = cum_x[L - 1:L, :]
        e_x = jnp.exp(cum_x)
        xs = xs_c[rows, :]
        xd = (xs * jnp.exp(cum_last - cum_x) * dt_x).astype(BF16)
        e_last = jnp.exp(cum_last)
        for g in range(SSD_GROUPS):
            gs = slice(g * SSD_GROUP_W, (g + 1) * SSD_GROUP_W)
            ns = slice(g * SSD_STATE, (g + 1) * SSD_STATE)
            bg = b_c[rows, ns]
            cg = c_c[rows, ns].astype(BF16)
            cb = lax.dot_general(cg, bg.astype(BF16), (((1,), (1,)), ((), ())),
                                 preferred_element_type=F32)
            st = state[:, gs]
            y_off = jnp.dot(cg, st.astype(BF16), preferred_element_type=F32) * e_x[:, gs]
            state[:, gs] = e_last[:, gs] * st + jnp.dot(bg.T.astype(BF16), xd[:, gs],
                                                        preferred_element_type=F32)
            for pr in range(SSD_GROUP_W // LANES):
                h0 = g * (SSD_HEADS // SSD_GROUPS) + 2 * pr
                ps = slice(h0 * SSD_HEADDIM, h0 * SSD_HEADDIM + LANES)
                x_pair = xs[:, ps].astype(BF16)
                ys = []
                for h in (h0, h0 + 1):
                    seg = cum[:, h:h + 1] - cum_t[h:h + 1, :]
                    decay = jnp.where(tri, jnp.exp(jnp.minimum(seg, 0.0)), 0.0)
                    wts = (cb * decay * dt_t[h:h + 1, :]).astype(BF16)
                    ys.append(jnp.dot(wts, x_pair, preferred_element_type=F32))
                y_pair = jnp.where(lane_lo, ys[0], ys[1])
                ybuf[:, ps] = y_pair + y_off[:, pr * LANES:(pr + 1) * LANES]
        y = ybuf[...] + xs * dx_ref[...]
        y = y * _silu(z_ref[rows, :].astype(F32))
        for g in range(SSD_GROUPS):
            gs = slice(g * SSD_GROUP_W, (g + 1) * SSD_GROUP_W)
            yg = y[:, gs]
            ms = jnp.mean(yg * yg, axis=-1, keepdims=True)
            o_ref[rows, gs] = (yg * lax.rsqrt(ms + RMS_EPS) * nw_ref[:, gs]).astype(o_ref.dtype)
        return carry

    lax.fori_loop(0, ts // L, chunk, 0)


def ssd_pre(u, dt_raw, conv_w, conv_b, dt_bias, a_log, d_skip, norm_w, *, ts=512):
    s = u.shape[0]
    ts = min(ts, s)
    hb = ts // SSD_HALO
    halo_idx = lambda i: jnp.maximum(i * hb - 1, 0)
    cw = jnp.zeros((8, conv_w.shape[1]), F32).at[:SSD_CONV_K].set(conv_w)
    n_bc = SSD_GROUPS * SSD_STATE
    cwx, cwb, cwc = cw[:, :SSD_INNER], cw[:, SSD_INNER:SSD_INNER + n_bc], cw[:, SSD_INNER + n_bc:]
    cb2 = conv_b.reshape(1, -1)
    cbx, cbb, cbc = cb2[:, :SSD_INNER], cb2[:, SSD_INNER:SSD_INNER + n_bc], cb2[:, SSD_INNER + n_bc:]
    pad_h = lambda v: jnp.zeros((1, DT_PAD), F32).at[0, :SSD_HEADS].set(v)
    dx = jnp.repeat(d_skip, SSD_HEADDIM).reshape(1, SSD_INNER)
    ex = (jnp.arange(SSD_INNER)[None, :] // SSD_HEADDIM == jnp.arange(DT_PAD)[:, None]).astype(BF16)
    ex = jnp.concatenate([ex, ex, ex], axis=0)
    const = lambda shape: pl.BlockSpec(shape, lambda i: (0, 0))
    return pl.pallas_call(
        _ssd_body,
        grid=(s // ts,),
        in_specs=[pl.BlockSpec((ts, SSD_INNER), lambda i: (i, C_X // SSD_INNER)),
                  pl.BlockSpec((ts, n_bc), lambda i: (i, C_B // n_bc)),
                  pl.BlockSpec((ts, n_bc), lambda i: (i, C_C // n_bc)),
                  pl.BlockSpec((ts, SSD_INNER), lambda i: (i, C_Z // SSD_INNER)),
                  pl.BlockSpec((SSD_HALO, SSD_INNER), lambda i: (halo_idx(i), C_X // SSD_INNER)),
                  pl.BlockSpec((SSD_HALO, n_bc), lambda i: (halo_idx(i), C_B // n_bc)),
                  pl.BlockSpec((SSD_HALO, n_bc), lambda i: (halo_idx(i), C_C // n_bc)),
                  pl.BlockSpec((ts, DT_PAD), lambda i: (i, 0)),
                  const((8, SSD_INNER)), const((8, n_bc)), const((8, n_bc)),
                  const((1, SSD_INNER)), const((1, n_bc)), const((1, n_bc)),
                  const((1, DT_PAD)), const((1, DT_PAD)), const((1, SSD_INNER)), const((1, SSD_INNER)),
                  const((3 * DT_PAD, SSD_INNER))],
        out_specs=pl.BlockSpec((ts, SSD_INNER), lambda i: (i, 0)),
        out_shape=jax.ShapeDtypeStruct((s, SSD_INNER), BF16),
        scratch_shapes=[pltpu.VMEM((SSD_STATE, SSD_INNER), F32),
                        pltpu.VMEM((SSD_HALO + ts, SSD_INNER), F32),
                        pltpu.VMEM((ts, SSD_INNER), F32),
                        pltpu.VMEM((ts, n_bc), F32),
                        pltpu.VMEM((ts, n_bc), F32),
                        pltpu.VMEM((SSD_CHUNK, SSD_INNER), F32),
                        pltpu.VMEM((ts, DT_PAD), F32), pltpu.VMEM((ts, DT_PAD), F32),
                        pltpu.VMEM((ts, SSD_INNER), F32), pltpu.VMEM((ts, SSD_INNER), F32)],
        compiler_params=_cparams(("arbitrary",)), name="ssd_pre",
    )(u, u, u, u, u, u, u, dt_raw, cwx, cwb, cwc, cbx, cbb, cbc,
      pad_h(dt_bias), pad_h(a_log), dx, norm_w.reshape(1, SSD_INNER), ex)


LOG2E = 1.4426950408889634
ATT_TQ = 1024
ATT_TK = ATT_TQ // 2
ATT_V_ROWS = DA_V_DIM + 16
ATT_NORM_SLACK = 1.02
ATT_MIN_ROW_SUM = 2.0 ** -64


def _rope_body(q_ref, k_ref, v_ref, pos_ref, inv_ref, qo_ref, ko_ref, vo_ref, kn_ref):
    ts = q_ref.shape[0]
    li = lax.broadcasted_iota(jnp.int32, (LANES, LANES), 0) // DA_HEAD_DIM
    lj = lax.broadcasted_iota(jnp.int32, (LANES, LANES), 1) // DA_HEAD_DIM
    same_comp = jnp.where(li == lj, 1.0, 0.0).astype(BF16)
    lane = lax.broadcasted_iota(jnp.int32, (ts, LANES), 1) % DA_HEAD_DIM
    half = ROPE_DIM // 2
    ang = pos_ref[...].astype(F32) * inv_ref[...]
    cos = jnp.cos(ang)
    sin = jnp.sin(ang)
    c_all = jnp.where(lane < ROPE_DIM, cos, 1.0)
    s_lo = jnp.where(lane < half, -sin, 0.0)
    s_hi = jnp.where((lane >= half) & (lane < ROPE_DIM), sin, 0.0)
    q_scale = DA_HEAD_DIM ** -0.5 * LOG2E

    def rotate(t):
        up = pltpu.roll(t, LANES - half, axis=1)
        dn = pltpu.roll(t, half, axis=1)
        return t * c_all + up * s_lo + dn * s_hi

    for c in range(q_ref.shape[1] // LANES):
        cs = slice(c * LANES, (c + 1) * LANES)
        kr = rotate(k_ref[:, cs].astype(F32)).astype(ko_ref.dtype)
        ko_ref[:, cs] = kr
        kf = kr.astype(F32)
        norm2 = jnp.dot((kf * kf).astype(BF16), same_comp, preferred_element_type=F32)
        kn_ref[c:c + 1, :] = jnp.sqrt(jnp.max(norm2, axis=0, keepdims=True)) * ATT_NORM_SLACK
        qo_ref[cs, :] = (rotate(q_ref[:, cs].astype(F32)) * q_scale).T.astype(qo_ref.dtype)
        vo_ref[c, 0:DA_V_DIM, :] = v_ref[:, cs].astype(F32).T.astype(vo_ref.dtype)
        extra = lax.broadcasted_iota(jnp.int32, (ATT_V_ROWS - DA_V_DIM, ts), 0) == 0
        vo_ref[c, DA_V_DIM:ATT_V_ROWS, :] = jnp.where(extra, 1.0, 0.0).astype(vo_ref.dtype)


def rope_prep(u, positions, *, ts=ATT_TK):
    s = u.shape[0]
    ts = min(ts, s)
    w = DA_HEADS * 2 * DA_HEAD_DIM
    inv = 1.0 / (ROPE_THETA ** (jnp.arange(0, ROPE_DIM, 2, dtype=F32) / ROPE_DIM))
    lane = jnp.arange(LANES) % DA_HEAD_DIM
    inv_lane = jnp.where(lane < ROPE_DIM, inv[lane % (ROPE_DIM // 2)], 0.0).reshape(1, LANES)
    return pl.pallas_call(
        _rope_body,
        grid=(s // ts,),
        in_specs=[pl.BlockSpec((ts, w), lambda i: (i, C_Q // w)),
                  pl.BlockSpec((ts, w), lambda i: (i, C_K // w)),
                  pl.BlockSpec((ts, w), lambda i: (i, C_V // w)),
                  pl.BlockSpec((ts, 1), lambda i: (i, 0)),
                  pl.BlockSpec((1, LANES), lambda i: (0, 0))],
        out_specs=[pl.BlockSpec((w, ts), lambda i: (0, i)), pl.BlockSpec((ts, w), lambda i: (i, 0)),
                   pl.BlockSpec((DA_HEADS, None, ATT_V_ROWS, ts), lambda i: (0, i, 0, 0)),
                   pl.BlockSpec((None, DA_HEADS, LANES), lambda i: (i, 0, 0))],
        out_shape=[jax.ShapeDtypeStruct((w, s), BF16), jax.ShapeDtypeStruct((s, w), BF16),
                   jax.ShapeDtypeStruct((DA_HEADS, s // ts, ATT_V_ROWS, ts), BF16),
                   jax.ShapeDtypeStruct((s // ts, DA_HEADS, LANES), F32)],
        compiler_params=_cparams(("arbitrary",)), name="rope_prep",
    )(u, u, u, positions.reshape(s, 1), inv_lane)


def _attn_body(qt_ref, k_ref, vt_ref, kn_ref, lq1_ref, lk1_ref, lq2_ref, lk2_ref, sw_ref, o_ref,
               m_ref, acc_ref, sa_ref, sb_ref, ma_ref, mb_ref, *, lambda_init):
    tq = qt_ref.shape[1]
    tk = vt_ref.shape[2]
    assert tq == 2 * tk
    i = pl.program_id(1)
    row = lax.broadcasted_iota(jnp.int32, (LANES, tq), 0)
    qt = qt_ref[...]
    zero = jnp.zeros_like(qt)
    qc = (jnp.where(row < DA_HEAD_DIM, qt, zero), jnp.where(row >= DA_HEAD_DIM, qt, zero))
    m_ref[...] = jnp.full(m_ref.shape, NEG_INF, F32)
    acc_ref[...] = jnp.zeros(acc_ref.shape, F32)

    def scores(j, s_ref, mc_ref):
        kb = k_ref[pl.ds(pl.multiple_of(j * tk, tk), tk), :]
        for c in range(2):
            s = jnp.dot(kb, qc[c], preferred_element_type=F32)
            s_ref[c] = s
            mc_ref[c] = jnp.max(s, axis=0, keepdims=True)

    def consume(j, s_ref, mc_ref, masked):
        vtb = vt_ref[j]
        for c in range(2):
            s = s_ref[c]
            if masked:
                kk = lax.broadcasted_iota(jnp.int32, (tk, tq), 0) + (j * tk - i * tq)
                qq = lax.broadcasted_iota(jnp.int32, (tk, tq), 1)
                s = jnp.where(kk <= qq, s, NEG_INF)
                m_cur = jnp.max(s, axis=0, keepdims=True)
            else:
                m_cur = mc_ref[c]
            m_prev = m_ref[c]
            m_new = jnp.maximum(m_prev, m_cur)
            alpha = jnp.exp2(m_prev - m_new)
            p = jnp.exp2(s - m_new).astype(BF16)
            acc_ref[c] = alpha * acc_ref[c] + jnp.dot(vtb, p, preferred_element_type=F32)
            m_ref[c] = m_new

    def pair(t, carry):
        j0 = 2 * t
        scores(j0 + 1, sb_ref, mb_ref)
        consume(j0, sa_ref, ma_ref, False)
        scores(j0 + 2, sa_ref, ma_ref)
        consume(j0 + 1, sb_ref, mb_ref, False)
        return carry

    qn = [jnp.sqrt(jnp.sum(jnp.square(qc[c].astype(F32)), axis=0, keepdims=True)) for c in range(2)]

    def stream(j, lo=0, hi=None, triangular=False):
        hi = tq if hi is None else hi
        qs = slice(lo, hi)
        kb = k_ref[pl.ds(pl.multiple_of(j * tk, tk), tk), :]
        vtb = vt_ref[j]
        kn = kn_ref[pl.ds(j, 1), :]
        for c in range(2):
            s = jnp.dot(kb, qc[c][:, qs], preferred_element_type=F32)
            if triangular:
                kk = lax.broadcasted_iota(jnp.int32, s.shape, 0)
                qq = lax.broadcasted_iota(jnp.int32, s.shape, 1)
                s = jnp.where(kk <= qq, s, NEG_INF)
            m_prev = m_ref[c, :, qs]
            m_new = jnp.maximum(m_prev, qn[c][:, qs] * kn[:, c * DA_HEAD_DIM:c * DA_HEAD_DIM + 1])
            alpha = jnp.exp2(m_prev - m_new)
            p = jnp.exp2(s - m_new).astype(BF16)
            acc_ref[c, :, qs] = alpha * acc_ref[c, :, qs] + jnp.dot(vtb, p, preferred_element_type=F32)
            m_ref[c, :, qs] = m_new

    def stream_pair(t, carry):
        stream(2 * t)
        stream(2 * t + 1)
        return carry

    lax.fori_loop(0, i, stream_pair, 0)
    stream(2 * i, 0, tk, triangular=True)
    stream(2 * i, tk, tq)
    stream(2 * i + 1, tk, tq, triangular=True)

    row_sums = jnp.minimum(acc_ref[0, DA_V_DIM:DA_V_DIM + 1, :], acc_ref[1, DA_V_DIM:DA_V_DIM + 1, :])
    healthy = jnp.min(row_sums) >= ATT_MIN_ROW_SUM

    @pl.when(jnp.logical_not(healthy))
    def _():
        m_ref[...] = jnp.full(m_ref.shape, NEG_INF, F32)
        acc_ref[...] = jnp.zeros(acc_ref.shape, F32)
        scores(0, sa_ref, ma_ref)
        lax.fori_loop(0, i, pair, 0)
        scores(2 * i + 1, sb_ref, mb_ref)
        consume(2 * i, sa_ref, ma_ref, True)
        consume(2 * i + 1, sb_ref, mb_ref, True)

    lam = (jnp.exp(jnp.sum(lq1_ref[...] * lk1_ref[...], axis=-1, keepdims=True))
           - jnp.exp(jnp.sum(lq2_ref[...] * lk2_ref[...], axis=-1, keepdims=True)) + lambda_init)
    nd = DA_V_DIM
    ot = (acc_ref[0, 0:nd, :] / acc_ref[0, nd:nd + 1, :]
          - lam * (acc_ref[1, 0:nd, :] / acc_ref[1, nd:nd + 1, :]))
    ms = jnp.mean(ot * ot, axis=0, keepdims=True)
    ot = ot * (lax.rsqrt(ms + LN_EPS) * (1.0 - lambda_init))
    o_ref[...] = (ot.T * sw_ref[...]).astype(o_ref.dtype)


def diff_attention(q_t, k_r, v_t, k_norm, lq1, lk1, lq2, lk2, subln_w, lambda_init, *, tq=ATT_TQ):
    s = k_r.shape[0]
    nkb, tk = v_t.shape[1], v_t.shape[3]
    tq = min(tq, s)
    row = lambda v: v.reshape(1, -1)
    const = lambda n: pl.BlockSpec((1, n), lambda h, i: (0, 0))
    return pl.pallas_call(
        functools.partial(_attn_body, lambda_init=lambda_init),
        grid=(DA_HEADS, s // tq),
        in_specs=[pl.BlockSpec((LANES, tq), lambda h, i: (h, i)),
                  pl.BlockSpec((s, LANES), lambda h, i: (0, h)),
                  pl.BlockSpec((None, nkb, ATT_V_ROWS, tk), lambda h, i: (h, 0, 0, 0)),
                  pl.BlockSpec((None, nkb, LANES), lambda h, i: (h, 0, 0)),
                  const(DA_HEAD_DIM), const(DA_HEAD_DIM), const(DA_HEAD_DIM), const(DA_HEAD_DIM),
                  const(DA_V_DIM)],
        out_specs=pl.BlockSpec((tq, LANES), lambda h, i: (i, h)),
        out_shape=jax.ShapeDtypeStruct((s, DA_HEADS * DA_V_DIM), BF16),
        scratch_shapes=[pltpu.VMEM((2, 1, tq), F32),
                        pltpu.VMEM((2, ATT_V_ROWS, tq), F32),
                        pltpu.VMEM((2, tk, tq), F32), pltpu.VMEM((2, tk, tq), F32),
                        pltpu.VMEM((2, 1, tq), F32), pltpu.VMEM((2, 1, tq), F32)],
        compiler_params=_cparams(("arbitrary", "arbitrary")), name="diff_attention",
    )(q_t, k_r, v_t, jnp.swapaxes(k_norm, 0, 1), row(lq1), row(lk1), row(lq2), row(lk2), row(subln_w))


def _merge_body(ha_ref, hb_ref, hc_ref, wa_ref, wb_ref, wc_ref, ga_ref, gb_ref, gc_ref, bias_ref, o_ref):
    def gated(h_ref, w_ref, g_ref, bi):
        y = jnp.dot(h_ref[...], w_ref[...], preferred_element_type=F32)
        return jax.nn.sigmoid(g_ref[...].astype(F32) + bias_ref[bi:bi + 1, :]) * y

    m = gated(ha_ref, wa_ref, ga_ref, 0) + gated(hb_ref, wb_ref, gb_ref, 1) + gated(hc_ref, wc_ref, gc_ref, 2)
    o_ref[...] = m.astype(o_ref.dtype)


def gated_merge(h_a, h_b, h_c, w_a, w_b, w_c, u, gate_b, layer, *, tm=1024, tn=512):
    s = h_a.shape[0]
    d = w_a.shape[2]
    tm = min(tm, s)
    wspec = lambda w: pl.BlockSpec((None, w.shape[1], tn), lambda i, j: (layer, 0, j))
    gblk = C_GATE // tn
    nblk = d // tn
    return pl.pallas_call(
        _merge_body,
        grid=(s // tm, d // tn),
        in_specs=[pl.BlockSpec((tm, h_a.shape[1]), lambda i, j: (i, 0)),
                  pl.BlockSpec((tm, h_b.shape[1]), lambda i, j: (i, 0)),
                  pl.BlockSpec((tm, h_c.shape[1]), lambda i, j: (i, 0)),
                  wspec(w_a), wspec(w_b), wspec(w_c),
                  pl.BlockSpec((tm, tn), lambda i, j: (i, gblk + j)),
                  pl.BlockSpec((tm, tn), lambda i, j: (i, gblk + nblk + j)),
                  pl.BlockSpec((tm, tn), lambda i, j: (i, gblk + 2 * nblk + j)),
                  pl.BlockSpec((3, tn), lambda i, j: (0, j))],
        out_specs=pl.BlockSpec((tm, tn), lambda i, j: (i, j)),
        out_shape=jax.ShapeDtypeStruct((s, d), BF16),
        compiler_params=_cparams(("arbitrary", "arbitrary")), name="gated_merge",
    )(h_a, h_b, h_c, w_a, w_b, w_c, u, u, u, gate_b)


def _residual_matmul_body(x_ref, a_ref, w_ref, o_ref):
    o_ref[...] = x_ref[...] + jnp.dot(a_ref[...], w_ref[...], preferred_element_type=F32)


def residual_matmul(x, a, w, layer, *, tm, tn, name):
    s, d = x.shape
    tm = min(tm, s)
    return pl.pallas_call(
        _residual_matmul_body,
        grid=(s // tm, d // tn),
        in_specs=[pl.BlockSpec((tm, tn), lambda i, j: (i, j)),
                  pl.BlockSpec((tm, a.shape[1]), lambda i, j: (i, 0)),
                  pl.BlockSpec((None, w.shape[1], tn), lambda i, j: (layer, 0, j))],
        out_specs=pl.BlockSpec((tm, tn), lambda i, j: (i, j)),
        out_shape=jax.ShapeDtypeStruct((s, d), F32),
        compiler_params=_cparams(("arbitrary", "arbitrary")), name=name,
    )(x, a, w)


def _rmsnorm_body(x_ref, w_ref, o_ref):
    x = x_ref[...]
    ms = jnp.mean(x * x, axis=-1, keepdims=True)
    o_ref[...] = x * lax.rsqrt(ms + RMS_EPS) * w_ref[...]


def rmsnorm(x, w, *, tm=512):
    s, d = x.shape
    tm = min(tm, s)
    return pl.pallas_call(
        _rmsnorm_body,
        grid=(s // tm,),
        in_specs=[pl.BlockSpec((tm, d), lambda i: (i, 0)), pl.BlockSpec((1, d), lambda i: (0, 0))],
        out_specs=pl.BlockSpec((tm, d), lambda i: (i, 0)),
        out_shape=jax.ShapeDtypeStruct((s, d), F32),
        compiler_params=_cparams(("arbitrary",)), name="final_rmsnorm",
    )(x, w.reshape(1, d))


FFN_HALO = 16


def _ffn_up_body(x_ref, xh_ref, nw_ref, wg_ref, wu_ref, cw_ref, o_ref, xn_ref, xnh_ref, ebuf, *, row_chunk):
    tm = x_ref.shape[0]
    i = pl.program_id(0)

    def normed(x):
        ms = jnp.mean(x * x, axis=-1, keepdims=True)
        return (x * lax.rsqrt(ms + RMS_EPS) * nw_ref[...]).astype(BF16)

    @pl.when(pl.program_id(1) == 0)
    def _():
        def chunk(c, carry):
            r0 = pl.multiple_of(c * row_chunk, row_chunk)
            xn_ref[pl.ds(r0, row_chunk), :] = normed(x_ref[pl.ds(r0, row_chunk), :])
            return carry
        lax.fori_loop(0, tm // row_chunk, chunk, 0)
        xnh_ref[...] = normed(xh_ref[...])

    gate_h = jnp.dot(xnh_ref[...], wg_ref[...], preferred_element_type=F32)
    ebuf[0:FFN_HALO, :] = jnp.where(i > 0, gate_h, 0.0)
    ebuf[FFN_HALO:, :] = jnp.dot(xn_ref[...], wg_ref[...], preferred_element_type=F32)
    up = jnp.dot(xn_ref[...], wu_ref[...], preferred_element_type=F32)
    base = FFN_HALO - (FFN_CONV_K - 1)
    acc = jnp.zeros(o_ref.shape, F32)
    for t in range(FFN_CONV_K):
        acc = acc + cw_ref[t:t + 1, :] * ebuf[base + t: base + t + tm, :]
    o_ref[...] = (_silu(acc) * up).astype(o_ref.dtype)


def ffn_up_act(x, nw, w_up, conv_w, layer, *, tm=1024, tn=512):
    s, d = x.shape
    tm = min(tm, s)
    nj = D_FF // tn
    hb = tm // FFN_HALO
    halo_idx = lambda i: jnp.maximum(i * hb - 1, 0)
    cw = jnp.zeros((8, D_FF), F32).at[:FFN_CONV_K].set(conv_w)
    return pl.pallas_call(
        functools.partial(_ffn_up_body, row_chunk=min(128, tm)),
        grid=(s // tm, nj),
        in_specs=[pl.BlockSpec((tm, d), lambda i, j: (i, 0)),
                  pl.BlockSpec((FFN_HALO, d), lambda i, j: (halo_idx(i), 0)),
                  pl.BlockSpec((1, d), lambda i, j: (0, 0)),
                  pl.BlockSpec((None, d, tn), lambda i, j: (layer, 0, j)),
                  pl.BlockSpec((None, d, tn), lambda i, j: (layer, 0, nj + j)),
                  pl.BlockSpec((8, tn), lambda i, j: (0, j))],
        out_specs=pl.BlockSpec((tm, tn), lambda i, j: (i, j)),
        out_shape=jax.ShapeDtypeStruct((s, D_FF), BF16),
        scratch_shapes=[pltpu.VMEM((tm, d), BF16), pltpu.VMEM((FFN_HALO, d), BF16),
                        pltpu.VMEM((FFN_HALO + tm, tn), F32)],
        compiler_params=_cparams(("arbitrary", "arbitrary")), name="ffn_up_act",
    )(x, x, nw.reshape(1, d), w_up, w_up, cw)


def kernel(x, positions, norm1_w, w_in, gate_b, conv_dw_w, conv_dw_b, conv_ln_w, conv_ln_b, conv_out_w,
           ssd_conv_w, ssd_conv_b, ssd_dt_bias, ssd_a_log, ssd_d, ssd_norm_w, ssd_out_w,
           da_lambda_q1, da_lambda_k1, da_lambda_q2, da_lambda_k2, da_subln_w, da_out_w,
           w_o, norm2_w, ffn_up_w, ffn_dw_w, ffn_down_w, final_norm_w):
    bsz, s_len, d = x.shape
    assert bsz == 1 and d == D_MODEL
    xc = x.reshape(s_len, d)
    pos = positions.reshape(s_len)
    w_in_bf = w_in.astype(BF16)
    w_tail = w_in_bf[:, :, DT_OFF + SSD_HEADS:]
    w_dt = jnp.zeros((DEPTH, d, DT_PAD), BF16).at[:, :, :SSD_HEADS].set(w_in_bf[:, :, DT_OFF:DT_OFF + SSD_HEADS])
    conv_out_bf, ssd_out_bf, da_out_bf = conv_out_w.astype(BF16), ssd_out_w.astype(BF16), da_out_w.astype(BF16)
    w_o_bf, ffn_up_bf, ffn_down_bf = w_o.astype(BF16), ffn_up_w.astype(BF16), ffn_down_w.astype(BF16)
    for l in range(DEPTH):
        lambda_init = 0.8 - 0.6 * math.exp(-0.3 * l)
        u, dt_raw = in_proj(xc, norm1_w[l], w_in_bf, w_tail, w_dt, l)
        h_a = conformer_pre(u, conv_dw_w[l], conv_dw_b[l], conv_ln_w[l], conv_ln_b[l])
        h_b = ssd_pre(u, dt_raw, ssd_conv_w[l], ssd_conv_b[l], ssd_dt_bias[l], ssd_a_log[l],
                      ssd_d[l], ssd_norm_w[l])
        q_t, k_r, v_t, k_norm = rope_prep(u, pos)
        h_c = diff_attention(q_t, k_r, v_t, k_norm, da_lambda_q1[l], da_lambda_k1[l], da_lambda_q2[l],
                             da_lambda_k2[l], da_subln_w[l], lambda_init)
        merged = gated_merge(h_a, h_b, h_c, conv_out_bf, ssd_out_bf, da_out_bf, u, gate_b[l], l)
        xc = residual_matmul(xc, merged, w_o_bf, l, tm=512, tn=d, name="wo_residual")
        act = ffn_up_act(xc, norm2_w[l], ffn_up_bf, ffn_dw_w[l], l)
        xc = residual_matmul(xc, act, ffn_down_bf, l, tm=1024, tn=256, name="ffn_down")
    return rmsnorm(xc, final_norm_w).reshape(bsz, s_len, d)
```

```python
import functools
import math

import jax
import jax.numpy as jnp
from jax import lax
from jax.experimental import pallas as pl
from jax.experimental.pallas import tpu as pltpu

F32 = jnp.float32
BF16 = jnp.bfloat16

D_MODEL = 2048
DEPTH = 2
CONV_CH = 1024
CONV_K = 31
SSD_HEADS = 32
SSD_HEADDIM = 64
SSD_INNER = SSD_HEADS * SSD_HEADDIM
SSD_GROUPS = 4
SSD_STATE = 128
SSD_CONV_K = 4
SSD_CHUNK = 128
SSD_GROUP_W = SSD_INNER // SSD_GROUPS
DA_HEADS = 8
DA_HEAD_DIM = 64
DA_V_DIM = 128
ROPE_DIM = 16
ROPE_THETA = 500000.0
D_FF = 5632
FFN_CONV_K = 3
RMS_EPS = 1e-6
LN_EPS = 1e-5
NEG_INF = -1e30

C_CONV_A = 0
C_CONV_G = 1024
C_Z = 2048
C_X = 4096
C_B = 6144
C_C = 6656
C_Q = 7168
C_K = 8192
C_V = 9216
C_GATE = 10240
U_WIDTH = 16384
DT_OFF = 7168
DT_PAD = 128

LANES = 128
VMEM_LIMIT = 56 * 1024 * 1024


def _cparams(sem, flags=None):
    return pltpu.CompilerParams(dimension_semantics=sem, vmem_limit_bytes=VMEM_LIMIT, flags=flags)


_NT = (((1,), (1,)), ((), ()))


def _in_proj_body(x_ref, nw_ref, w_ref, ws_ref, o_ref, os_ref, xn_ref, *, row_chunk):
    tm = x_ref.shape[0]

    @pl.when(pl.program_id(1) == 0)
    def _():
        def chunk(c, carry):
            r0 = pl.multiple_of(c * row_chunk, row_chunk)
            x = x_ref[pl.ds(r0, row_chunk), :]
            ms = jnp.mean(x * x, axis=-1, keepdims=True)
            xn_ref[pl.ds(r0, row_chunk), :] = (x * lax.rsqrt(ms + RMS_EPS) * nw_ref[...]).astype(BF16)
            return carry
        lax.fori_loop(0, tm // row_chunk, chunk, 0)
        os_ref[...] = lax.dot_general(xn_ref[...], ws_ref[...], _NT, preferred_element_type=F32)

    o_ref[...] = lax.dot_general(xn_ref[...], w_ref[0], _NT, preferred_element_type=F32).astype(o_ref.dtype)


def in_proj(x, nw, w_t, layer, *, tm=1024, tn=1024):
    s, d = x.shape
    tm = min(tm, s)
    n_blocks = U_WIDTH // tn
    row0 = lambda j: pl.multiple_of(j * tn + jnp.where(j * tn >= DT_OFF, SSD_HEADS, 0), SSD_HEADS)
    return pl.pallas_call(
        functools.partial(_in_proj_body, row_chunk=min(128, tm)),
        grid=(s // tm, n_blocks),
        in_specs=[pl.BlockSpec((tm, d), lambda i, j: (i, 0)),
                  pl.BlockSpec((1, d), lambda i, j: (0, 0)),
                  pl.BlockSpec((pl.Element(1), pl.Element(tn), pl.Element(d)), lambda i, j: (layer, row0(j), 0)),
                  pl.BlockSpec((None, DT_PAD, d), lambda i, j: (layer, DT_OFF // DT_PAD, 0))],
        out_specs=[pl.BlockSpec((tm, tn), lambda i, j: (i, j)),
                   pl.BlockSpec((tm, DT_PAD), lambda i, j: (i, 0))],
        out_shape=[jax.ShapeDtypeStruct((s, U_WIDTH), BF16), jax.ShapeDtypeStruct((s, DT_PAD), F32)],
        scratch_shapes=[pltpu.VMEM((tm, d), BF16)],
        compiler_params=_cparams(("arbitrary", "arbitrary")), name="in_proj",
    )(x, nw.reshape(1, d), w_t, w_t)


CONV_HALO = 32
CONV_ROWS = 128


def _conformer_body(a_ref, g_ref, ah_ref, gh_ref, w_ref, b_ref, lnw_ref, lnb_ref, o_ref, hbuf, cbuf):
    ts = a_ref.shape[0]
    i = pl.program_id(0)
    halo = ah_ref[...].astype(F32) * jax.nn.sigmoid(gh_ref[...].astype(F32))
    hbuf[0:CONV_HALO, :] = jnp.where(i > 0, halo, 0.0)
    hbuf[CONV_HALO:, :] = a_ref[...].astype(F32) * jax.nn.sigmoid(g_ref[...].astype(F32))
    base = CONV_HALO - (CONV_K - 1)
    sub = 8
    rows = CONV_ROWS
    ext = rows + CONV_HALO
    for c in range(CONV_CH // LANES):
        cs = slice(c * LANES, (c + 1) * LANES)
        for r0 in range(0, ts, rows):
            slab = hbuf[r0:r0 + ext, cs]
            acc = jnp.zeros((rows, LANES), F32)
            for res in range(sub):
                taps = [k for k in range(CONV_K) if (base + k) % sub == res]
                if not taps:
                    continue
                rolled = slab if res == 0 else pltpu.roll(slab, ext - res, axis=0)
                for k in taps:
                    a = (base + k) // sub * sub
                    acc = acc + w_ref[k:k + 1, cs] * rolled[a:a + rows]
            cbuf[r0:r0 + rows, cs] = acc + b_ref[:, cs]
    y = cbuf[...]
    mu = jnp.mean(y, axis=-1, keepdims=True)
    yc = y - mu
    var = jnp.mean(yc * yc, axis=-1, keepdims=True)
    yn = yc * lax.rsqrt(var + LN_EPS) * lnw_ref[...] + lnb_ref[...]
    o_ref[...] = (yn * jax.nn.sigmoid(yn)).astype(o_ref.dtype)


def conformer_pre(u, w_dw, b_dw, ln_w, ln_b, *, ts=256):
    s = u.shape[0]
    ts = min(ts, s)
    hb = ts // CONV_HALO
    w_pad = jnp.zeros((32, CONV_CH), F32).at[:CONV_K].set(w_dw)
    row = lambda v: v.reshape(1, CONV_CH)
    halo_idx = lambda i: jnp.maximum(i * hb - 1, 0)
    return pl.pallas_call(
        _conformer_body,
        grid=(s // ts,),
        in_specs=[pl.BlockSpec((ts, CONV_CH), lambda i: (i, C_CONV_A // CONV_CH)),
                  pl.BlockSpec((ts, CONV_CH), lambda i: (i, C_CONV_G // CONV_CH)),
                  pl.BlockSpec((CONV_HALO, CONV_CH), lambda i: (halo_idx(i), C_CONV_A // CONV_CH)),
                  pl.BlockSpec((CONV_HALO, CONV_CH), lambda i: (halo_idx(i), C_CONV_G // CONV_CH)),
                  pl.BlockSpec((32, CONV_CH), lambda i: (0, 0)),
                  pl.BlockSpec((1, CONV_CH), lambda i: (0, 0)),
                  pl.BlockSpec((1, CONV_CH), lambda i: (0, 0)),
                  pl.BlockSpec((1, CONV_CH), lambda i: (0, 0))],
        out_specs=pl.BlockSpec((ts, CONV_CH), lambda i: (i, 0)),
        out_shape=jax.ShapeDtypeStruct((s, CONV_CH), BF16),
        scratch_shapes=[pltpu.VMEM((CONV_HALO + ts, CONV_CH), F32), pltpu.VMEM((ts, CONV_CH), F32)],
        compiler_params=_cparams(("arbitrary",)), name="conformer_pre",
    )(u, u, u, u, w_pad, row(b_dw), row(ln_w), row(ln_b))


SSD_HALO = 8
SSD_CONV_ROWS = 64
SSD_CONV_LANES = 256


def _split3(v):
    hi = v.astype(BF16)
    r1 = v - hi.astype(F32)
    mid = r1.astype(BF16)
    lo = (r1 - mid.astype(F32)).astype(BF16)
    return hi, mid, lo


def _dot01_left(m01, v):
    hi, mid, lo = _split3(v)
    d = lambda p: jnp.dot(m01, p, preferred_element_type=F32)
    return d(hi) + d(mid) + d(lo)


def _softplus(x):
    return jnp.maximum(x, 0.0) + jnp.log1p(jnp.exp(-jnp.abs(x)))


def _silu(x):
    return x * jax.nn.sigmoid(x)


def _ssd_body(x_ref, b_ref, c_ref, z_ref, xh_ref, bh_ref, ch_ref, dt_ref,
              cwx_ref, cwb_ref, cwc_ref, cbx_ref, cbb_ref, cbc_ref,
              dtb_ref, alog_ref, dx_ref, nw_ref, ex_ref,
              o_ref, state, ebuf, xs_c, b_c, c_c, ybuf, dt_s, cum_s, cx_s, dtx_s):
    ts = x_ref.shape[0]
    L = SSD_CHUNK
    i = pl.program_id(0)

    @pl.when(i == 0)
    def _():
        state[...] = jnp.zeros_like(state)

    def conv(in_ref, halo_ref, w_ref, bias_ref, out_ref):
        width = in_ref.shape[1]
        ebuf[0:SSD_HALO, 0:width] = jnp.where(i > 0, halo_ref[...].astype(F32), 0.0)
        ebuf[SSD_HALO:, 0:width] = in_ref[...].astype(F32)
        base = SSD_HALO - (SSD_CONV_K - 1)
        rb, cw = SSD_CONV_ROWS, SSD_CONV_LANES
        ext = rb + SSD_HALO

        def row_block(bi, carry):
            r0 = pl.multiple_of(bi * rb, rb)
            for c in range(width // cw):
                cs = slice(c * cw, (c + 1) * cw)
                slab = ebuf[pl.ds(r0, ext), cs]
                acc = bias_ref[:, cs]
                for k in range(SSD_CONV_K):
                    a, res = (base + k) // 8 * 8, (base + k) % 8
                    rolled = slab if res == 0 else pltpu.roll(slab, ext - res, axis=0)
                    acc = acc + w_ref[k:k + 1, cs] * rolled[a:a + rb]
                out_ref[pl.ds(r0, rb), cs] = _silu(acc)
            return carry

        lax.fori_loop(0, ts // rb, row_block, 0)

    conv(x_ref, xh_ref, cwx_ref, cbx_ref, xs_c)
    conv(b_ref, bh_ref, cwb_ref, cbb_ref, b_c)
    conv(c_ref, ch_ref, cwc_ref, cbc_ref, c_c)

    a_row = -jnp.exp(alog_ref[...])
    rr = lax.broadcasted_iota(jnp.int32, (L, L), 0)
    cc = lax.broadcasted_iota(jnp.int32, (L, L), 1)
    tri = rr >= cc
    tri01 = jnp.where(tri, 1.0, 0.0).astype(BF16)
    lane_lo = lax.broadcasted_iota(jnp.int32, (L, LANES), 1) < SSD_HEADDIM

    dt_all = _softplus(dt_ref[...] + dtb_ref[...])
    dt_s[...] = dt_all
    for ci in range(ts // L):
        cum_s[ci * L:(ci + 1) * L, :] = _dot01_left(tri01, dt_all[ci * L:(ci + 1) * L] * a_row)
    for src, dst in ((cum_s, cx_s), (dt_s, dtx_s)):
        dst[...] = jnp.dot(jnp.concatenate(_split3(src[...]), axis=1), ex_ref[...],
                           preferred_element_type=F32)

    def chunk(ci, carry):
        r0 = pl.multiple_of(ci * L, L)
        rows = pl.ds(r0, L)
        dt = dt_s[rows, :]
        cum = cum_s[rows, :]
        cum_t = cum.T
        dt_t = dt.T
        cum_x = cx_s[rows, :]
        dt_x = dtx_s[rows, :]
        cum_last = cum_x[L - 1:L, :]
        e_x = jnp.exp(cum_x)
        xs = xs_c[rows, :]
        xd = (xs * jnp.exp(cum_last - cum_x) * dt_x).astype(BF16)
        e_last = jnp.exp(cum_last)
        for g in range(SSD_GROUPS):
            gs = slice(g * SSD_GROUP_W, (g + 1) * SSD_GROUP_W)
            ns = slice(g * SSD_STATE, (g + 1) * SSD_STATE)
            bg = b_c[rows, ns]
            cg = c_c[rows, ns].astype(BF16)
            cb = lax.dot_general(cg, bg.astype(BF16), (((1,), (1,)), ((), ())),
                                 preferred_element_type=F32)
            st = state[:, gs]
            y_off = jnp.dot(cg, st.astype(BF16), preferred_element_type=F32) * e_x[:, gs]
            state[:, gs] = e_last[:, gs] * st + jnp.dot(bg.T.astype(BF16), xd[:, gs],
                                                        preferred_element_type=F32)
            for pr in range(SSD_GROUP_W // LANES):
                h0 = g * (SSD_HEADS // SSD_GROUPS) + 2 * pr
                ps = slice(h0 * SSD_HEADDIM, h0 * SSD_HEADDIM + LANES)
                x_pair = xs[:, ps].astype(BF16)
                ys = []
                for h in (h0, h0 + 1):
                    seg = cum[:, h:h + 1] - cum_t[h:h + 1, :]
                    decay = jnp.where(tri, jnp.exp(jnp.minimum(seg, 0.0)), 0.0)
                    wts = (cb * decay * dt_t[h:h + 1, :]).astype(BF16)
                    ys.append(jnp.dot(wts, x_pair, preferred_element_type=F32))
                y_pair = jnp.where(lane_lo, ys[0], ys[1])
                ybuf[:, ps] = y_pair + y_off[:, pr * LANES:(pr + 1) * LANES]
        y = ybuf[...] + xs * dx_ref[...]
        y = y * _silu(z_ref[rows, :].astype(F32))
        for g in range(SSD_GROUPS):
            gs = slice(g * SSD_GROUP_W, (g + 1) * SSD_GROUP_W)
            yg = y[:, gs]
            ms = jnp.mean(yg * yg, axis=-1, keepdims=True)
            o_ref[rows, gs] = (yg * lax.rsqrt(ms + RMS_EPS) * nw_ref[:, gs]).astype(o_ref.dtype)
        return carry

    lax.fori_loop(0, ts // L, chunk, 0)


def ssd_pre(u, dt_raw, conv_w, conv_b, dt_bias, a_log, d_skip, norm_w, *, ts=512):
    s = u.shape[0]
    ts = min(ts, s)
    hb = ts // SSD_HALO
    halo_idx = lambda i: jnp.maximum(i * hb - 1, 0)
    cw = jnp.zeros((8, conv_w.shape[1]), F32).at[:SSD_CONV_K].set(conv_w)
    n_bc = SSD_GROUPS * SSD_STATE
    cwx, cwb, cwc = cw[:, :SSD_INNER], cw[:, SSD_INNER:SSD_INNER + n_bc], cw[:, SSD_INNER + n_bc:]
    cb2 = conv_b.reshape(1, -1)
    cbx, cbb, cbc = cb2[:, :SSD_INNER], cb2[:, SSD_INNER:SSD_INNER + n_bc], cb2[:, SSD_INNER + n_bc:]
    pad_h = lambda v: jnp.zeros((1, DT_PAD), F32).at[0, :SSD_HEADS].set(v)
    dx = jnp.repeat(d_skip, SSD_HEADDIM).reshape(1, SSD_INNER)
    ex = (jnp.arange(SSD_INNER)[None, :] // SSD_HEADDIM == jnp.arange(DT_PAD)[:, None]).astype(BF16)
    ex = jnp.concatenate([ex, ex, ex], axis=0)
    const = lambda shape: pl.BlockSpec(shape, lambda i: (0, 0))
    return pl.pallas_call(
        _ssd_body,
        grid=(s // ts,),
        in_specs=[pl.BlockSpec((ts, SSD_INNER), lambda i: (i, C_X // SSD_INNER)),
                  pl.BlockSpec((ts, n_bc), lambda i: (i, C_B // n_bc)),
                  pl.BlockSpec((ts, n_bc), lambda i: (i, C_C // n_bc)),
                  pl.BlockSpec((ts, SSD_INNER), lambda i: (i, C_Z // SSD_INNER)),
                  pl.BlockSpec((SSD_HALO, SSD_INNER), lambda i: (halo_idx(i), C_X // SSD_INNER)),
                  pl.BlockSpec((SSD_HALO, n_bc), lambda i: (halo_idx(i), C_B // n_bc)),
                  pl.BlockSpec((SSD_HALO, n_bc), lambda i: (halo_idx(i), C_C // n_bc)),
                  pl.BlockSpec((ts, DT_PAD), lambda i: (i, 0)),
                  const((8, SSD_INNER)), const((8, n_bc)), const((8, n_bc)),
                  const((1, SSD_INNER)), const((1, n_bc)), const((1, n_bc)),
                  const((1, DT_PAD)), const((1, DT_PAD)), const((1, SSD_INNER)), const((1, SSD_INNER)),
                  const((3 * DT_PAD, SSD_INNER))],
        out_specs=pl.BlockSpec((ts, SSD_INNER), lambda i: (i, 0)),
        out_shape=jax.ShapeDtypeStruct((s, SSD_INNER), BF16),
        scratch_shapes=[pltpu.VMEM((SSD_STATE, SSD_INNER), F32),
                        pltpu.VMEM((SSD_HALO + ts, SSD_INNER), F32),
                        pltpu.VMEM((ts, SSD_INNER), F32),
                        pltpu.VMEM((ts, n_bc), F32),
                        pltpu.VMEM((ts, n_bc), F32),
                        pltpu.VMEM((SSD_CHUNK, SSD_INNER), F32),
                        pltpu.VMEM((ts, DT_PAD), F32), pltpu.VMEM((ts, DT_PAD), F32),
                        pltpu.VMEM((ts, SSD_INNER), F32), pltpu.VMEM((ts, SSD_INNER), F32)],
        compiler_params=_cparams(("arbitrary",)), name="ssd_pre",
    )(u, u, u, u, u, u, u, dt_raw, cwx, cwb, cwc, cbx, cbb, cbc,
      pad_h(dt_bias), pad_h(a_log), dx, norm_w.reshape(1, SSD_INNER), ex)


LOG2E = 1.4426950408889634
ATT_TQ = 1024
ATT_TK = ATT_TQ // 2
ATT_V_ROWS = DA_V_DIM + 16
ATT_NORM_SLACK = 1.02
ATT_MIN_ROW_SUM = 2.0 ** -64


def _rope_body(q_ref, k_ref, v_ref, pos_ref, inv_ref, qo_ref, ko_ref, vo_ref, kn_ref):
    ts = q_ref.shape[0]
    li = lax.broadcasted_iota(jnp.int32, (LANES, LANES), 0) // DA_HEAD_DIM
    lj = lax.broadcasted_iota(jnp.int32, (LANES, LANES), 1) // DA_HEAD_DIM
    same_comp = jnp.where(li == lj, 1.0, 0.0).astype(BF16)
    lane = lax.broadcasted_iota(jnp.int32, (ts, LANES), 1) % DA_HEAD_DIM
    half = ROPE_DIM // 2
    ang = pos_ref[...].astype(F32) * inv_ref[...]
    cos = jnp.cos(ang)
    sin = jnp.sin(ang)
    c_all = jnp.where(lane < ROPE_DIM, cos, 1.0)
    s_lo = jnp.where(lane < half, -sin, 0.0)
    s_hi = jnp.where((lane >= half) & (lane < ROPE_DIM), sin, 0.0)
    q_scale = DA_HEAD_DIM ** -0.5 * LOG2E

    def rotate(t):
        up = pltpu.roll(t, LANES - half, axis=1)
        dn = pltpu.roll(t, half, axis=1)
        return t * c_all + up * s_lo + dn * s_hi

    for c in range(q_ref.shape[1] // LANES):
        cs = slice(c * LANES, (c + 1) * LANES)
        kr = rotate(k_ref[:, cs].astype(F32)).astype(ko_ref.dtype)
        ko_ref[:, cs] = kr
        kf = kr.astype(F32)
        norm2 = jnp.dot((kf * kf).astype(BF16), same_comp, preferred_element_type=F32)
        kn_ref[c:c + 1, :] = jnp.sqrt(jnp.max(norm2, axis=0, keepdims=True)) * ATT_NORM_SLACK
        qo_ref[cs, :] = (rotate(q_ref[:, cs].astype(F32)) * q_scale).T.astype(qo_ref.dtype)
        vo_ref[c, 0:DA_V_DIM, :] = v_ref[:, cs].astype(F32).T.astype(vo_ref.dtype)
        extra = lax.broadcasted_iota(jnp.int32, (ATT_V_ROWS - DA_V_DIM, ts), 0) == 0
        vo_ref[c, DA_V_DIM:ATT_V_ROWS, :] = jnp.where(extra, 1.0, 0.0).astype(vo_ref.dtype)


def rope_prep(u, positions, *, ts=ATT_TK):
    s = u.shape[0]
    ts = min(ts, s)
    w = DA_HEADS * 2 * DA_HEAD_DIM
    inv = 1.0 / (ROPE_THETA ** (jnp.arange(0, ROPE_DIM, 2, dtype=F32) / ROPE_DIM))
    lane = jnp.arange(LANES) % DA_HEAD_DIM
    inv_lane = jnp.where(lane < ROPE_DIM, inv[lane % (ROPE_DIM // 2)], 0.0).reshape(1, LANES)
    return pl.pallas_call(
        _rope_body,
        grid=(s // ts,),
        in_specs=[pl.BlockSpec((ts, w), lambda i: (i, C_Q // w)),
                  pl.BlockSpec((ts, w), lambda i: (i, C_K // w)),
                  pl.BlockSpec((ts, w), lambda i: (i, C_V // w)),
                  pl.BlockSpec((ts, 1), lambda i: (i, 0)),
                  pl.BlockSpec((1, LANES), lambda i: (0, 0))],
        out_specs=[pl.BlockSpec((w, ts), lambda i: (0, i)), pl.BlockSpec((ts, w), lambda i: (i, 0)),
                   pl.BlockSpec((DA_HEADS, None, ATT_V_ROWS, ts), lambda i: (0, i, 0, 0)),
                   pl.BlockSpec((None, DA_HEADS, LANES), lambda i: (i, 0, 0))],
        out_shape=[jax.ShapeDtypeStruct((w, s), BF16), jax.ShapeDtypeStruct((s, w), BF16),
                   jax.ShapeDtypeStruct((DA_HEADS, s // ts, ATT_V_ROWS, ts), BF16),
                   jax.ShapeDtypeStruct((s // ts, DA_HEADS, LANES), F32)],
        compiler_params=_cparams(("arbitrary",)), name="rope_prep",
    )(u, u, u, positions.reshape(s, 1), inv_lane)


def _attn_body(qt_ref, k_ref, vt_ref, kn_ref, lq1_ref, lk1_ref, lq2_ref, lk2_ref, sw_ref, o_ref,
               m_ref, acc_ref, sa_ref, sb_ref, ma_ref, mb_ref, *, lambda_init):
    tq = qt_ref.shape[1]
    tk = vt_ref.shape[2]
    assert tq == 2 * tk
    i = pl.program_id(1)
    row = lax.broadcasted_iota(jnp.int32, (LANES, tq), 0)
    qt = qt_ref[...]
    zero = jnp.zeros_like(qt)
    qc = (jnp.where(row < DA_HEAD_DIM, qt, zero), jnp.where(row >= DA_HEAD_DIM, qt, zero))
    m_ref[...] = jnp.full(m_ref.shape, NEG_INF, F32)
    acc_ref[...] = jnp.zeros(acc_ref.shape, F32)

    def scores(j, s_ref, mc_ref):
        kb = k_ref[pl.ds(pl.multiple_of(j * tk, tk), tk), :]
        for c in range(2):
            s = jnp.dot(kb, qc[c], preferred_element_type=F32)
            s_ref[c] = s
            mc_ref[c] = jnp.max(s, axis=0, keepdims=True)

    def consume(j, s_ref, mc_ref, masked):
        vtb = vt_ref[j]
        for c in range(2):
            s = s_ref[c]
            if masked:
                kk = lax.broadcasted_iota(jnp.int32, (tk, tq), 0) + (j * tk - i * tq)
                qq = lax.broadcasted_iota(jnp.int32, (tk, tq), 1)
                s = jnp.where(kk <= qq, s, NEG_INF)
                m_cur = jnp.max(s, axis=0, keepdims=True)
            else:
                m_cur = mc_ref[c]
            m_prev = m_ref[c]
            m_new = jnp.maximum(m_prev, m_cur)
            alpha = jnp.exp2(m_prev - m_new)
            p = jnp.exp2(s - m_new).astype(BF16)
            acc_ref[c] = alpha * acc_ref[c] + jnp.dot(vtb, p, preferred_element_type=F32)
            m_ref[c] = m_new

    def pair(t, carry):
        j0 = 2 * t
        scores(j0 + 1, sb_ref, mb_ref)
        consume(j0, sa_ref, ma_ref, False)
        scores(j0 + 2, sa_ref, ma_ref)
        consume(j0 + 1, sb_ref, mb_ref, False)
        return carry

    qn = [jnp.sqrt(jnp.sum(jnp.square(qc[c].astype(F32)), axis=0, keepdims=True)) for c in range(2)]

    def stream(j, lo=0, hi=None, triangular=False):
        hi = tq if hi is None else hi
        qs = slice(lo, hi)
        kb = k_ref[pl.ds(pl.multiple_of(j * tk, tk), tk), :]
        vtb = vt_ref[j]
        kn = kn_ref[pl.ds(j, 1), :]
        for c in range(2):
            s = jnp.dot(kb, qc[c][:, qs], preferred_element_type=F32)
            if triangular:
                kk = lax.broadcasted_iota(jnp.int32, s.shape, 0)
                qq = lax.broadcasted_iota(jnp.int32, s.shape, 1)
                s = jnp.where(kk <= qq, s, NEG_INF)
            m_prev = m_ref[c, :, qs]
            m_new = jnp.maximum(m_prev, qn[c][:, qs] * kn[:, c * DA_HEAD_DIM:c * DA_HEAD_DIM + 1])
            alpha = jnp.exp2(m_prev - m_new)
            p = jnp.exp2(s - m_new).astype(BF16)
            acc_ref[c, :, qs] = alpha * acc_ref[c, :, qs] + jnp.dot(vtb, p, preferred_element_type=F32)
            m_ref[c, :, qs] = m_new

    def stream_pair(t, carry):
        stream(2 * t)
        stream(2 * t + 1)
        return carry

    lax.fori_loop(0, i, stream_pair, 0)
    stream(2 * i, 0, tk, triangular=True)
    stream(2 * i, tk, tq)
    stream(2 * i + 1, tk, tq, triangular=True)

    row_sums = jnp.minimum(acc_ref[0, DA_V_DIM:DA_V_DIM + 1, :], acc_ref[1, DA_V_DIM:DA_V_DIM + 1, :])
    healthy = jnp.min(row_sums) >= ATT_MIN_ROW_SUM

    @pl.when(jnp.logical_not(healthy))
    def _():
        m_ref[...] = jnp.full(m_ref.shape, NEG_INF, F32)
        acc_ref[...] = jnp.zeros(acc_ref.shape, F32)
        scores(0, sa_ref, ma_ref)
        lax.fori_loop(0, i, pair, 0)
        scores(2 * i + 1, sb_ref, mb_ref)
        consume(2 * i, sa_ref, ma_ref, True)
        consume(2 * i + 1, sb_ref, mb_ref, True)

    lam = (jnp.exp(jnp.sum(lq1_ref[...] * lk1_ref[...], axis=-1, keepdims=True))
           - jnp.exp(jnp.sum(lq2_ref[...] * lk2_ref[...], axis=-1, keepdims=True)) + lambda_init)
    nd = DA_V_DIM
    ot = (acc_ref[0, 0:nd, :] / acc_ref[0, nd:nd + 1, :]
          - lam * (acc_ref[1, 0:nd, :] / acc_ref[1, nd:nd + 1, :]))
    ms = jnp.mean(ot * ot, axis=0, keepdims=True)
    ot = ot * (lax.rsqrt(ms + LN_EPS) * (1.0 - lambda_init))
    o_ref[...] = (ot.T * sw_ref[...]).astype(o_ref.dtype)


def diff_attention(q_t, k_r, v_t, k_norm, lq1, lk1, lq2, lk2, subln_w, lambda_init, *, tq=ATT_TQ):
    s = k_r.shape[0]
    nkb, tk = v_t.shape[1], v_t.shape[3]
    tq = min(tq, s)
    row = lambda v: v.reshape(1, -1)
    const = lambda n: pl.BlockSpec((1, n), lambda h, i: (0, 0))
    return pl.pallas_call(
        functools.partial(_attn_body, lambda_init=lambda_init),
        grid=(DA_HEADS, s // tq),
        in_specs=[pl.BlockSpec((LANES, tq), lambda h, i: (h, i)),
                  pl.BlockSpec((s, LANES), lambda h, i: (0, h)),
                  pl.BlockSpec((None, nkb, ATT_V_ROWS, tk), lambda h, i: (h, 0, 0, 0)),
                  pl.BlockSpec((None, nkb, LANES), lambda h, i: (h, 0, 0)),
                  const(DA_HEAD_DIM), const(DA_HEAD_DIM), const(DA_HEAD_DIM), const(DA_HEAD_DIM),
                  const(DA_V_DIM)],
        out_specs=pl.BlockSpec((tq, LANES), lambda h, i: (i, h)),
        out_shape=jax.ShapeDtypeStruct((s, DA_HEADS * DA_V_DIM), BF16),
        scratch_shapes=[pltpu.VMEM((2, 1, tq), F32),
                        pltpu.VMEM((2, ATT_V_ROWS, tq), F32),
                        pltpu.VMEM((2, tk, tq), F32), pltpu.VMEM((2, tk, tq), F32),
                        pltpu.VMEM((2, 1, tq), F32), pltpu.VMEM((2, 1, tq), F32)],
        compiler_params=_cparams(("arbitrary", "arbitrary")), name="diff_attention",
    )(q_t, k_r, v_t, jnp.swapaxes(k_norm, 0, 1), row(lq1), row(lk1), row(lq2), row(lk2), row(subln_w))


def _merge_body(ha_ref, hb_ref, hc_ref, wa_ref, wb_ref, wc_ref, ga_ref, gb_ref, gc_ref, bias_ref, o_ref):
    def gated(h_ref, w_ref, g_ref, bi):
        y = jnp.dot(h_ref[...], w_ref[...], preferred_element_type=F32)
        return jax.nn.sigmoid(g_ref[...].astype(F32) + bias_ref[bi:bi + 1, :]) * y

    m = gated(ha_ref, wa_ref, ga_ref, 0) + gated(hb_ref, wb_ref, gb_ref, 1) + gated(hc_ref, wc_ref, gc_ref, 2)
    o_ref[...] = m.astype(o_ref.dtype)


def gated_merge(h_a, h_b, h_c, w_a, w_b, w_c, u, gate_b, layer, *, tm=1024, tn=512):
    s = h_a.shape[0]
    d = w_a.shape[2]
    tm = min(tm, s)
    wspec = lambda w: pl.BlockSpec((None, w.shape[1], tn), lambda i, j: (layer, 0, j))
    gblk = C_GATE // tn
    nblk = d // tn
    return pl.pallas_call(
        _merge_body,
        grid=(s // tm, d // tn),
        in_specs=[pl.BlockSpec((tm, h_a.shape[1]), lambda i, j: (i, 0)),
                  pl.BlockSpec((tm, h_b.shape[1]), lambda i, j: (i, 0)),
                  pl.BlockSpec((tm, h_c.shape[1]), lambda i, j: (i, 0)),
                  wspec(w_a), wspec(w_b), wspec(w_c),
                  pl.BlockSpec((tm, tn), lambda i, j: (i, gblk + j)),
                  pl.BlockSpec((tm, tn), lambda i, j: (i, gblk + nblk + j)),
                  pl.BlockSpec((tm, tn), lambda i, j: (i, gblk + 2 * nblk + j)),
                  pl.BlockSpec((3, tn), lambda i, j: (0, j))],
        out_specs=pl.BlockSpec((tm, tn), lambda i, j: (i, j)),
        out_shape=jax.ShapeDtypeStruct((s, d), BF16),
        compiler_params=_cparams(("arbitrary", "arbitrary")), name="gated_merge",
    )(h_a, h_b, h_c, w_a, w_b, w_c, u, u, u, gate_b)


def _residual_matmul_body(x_ref, a_ref, w_ref, o_ref):
    o_ref[...] = x_ref[...] + jnp.dot(a_ref[...], w_ref[...], preferred_element_type=F32)


def residual_matmul(x, a, w, layer, *, tm, tn, name):
    s, d = x.shape
    tm = min(tm, s)
    return pl.pallas_call(
        _residual_matmul_body,
        grid=(s // tm, d // tn),
        in_specs=[pl.BlockSpec((tm, tn), lambda i, j: (i, j)),
                  pl.BlockSpec((tm, a.shape[1]), lambda i, j: (i, 0)),
                  pl.BlockSpec((None, w.shape[1], tn), lambda i, j: (layer, 0, j))],
        out_specs=pl.BlockSpec((tm, tn), lambda i, j: (i, j)),
        out_shape=jax.ShapeDtypeStruct((s, d), F32),
        compiler_params=_cparams(("arbitrary", "arbitrary")), name=name,
    )(x, a, w)


def _rmsnorm_body(x_ref, w_ref, o_ref):
    x = x_ref[...]
    ms = jnp.mean(x * x, axis=-1, keepdims=True)
    o_ref[...] = x * lax.rsqrt(ms + RMS_EPS) * w_ref[...]


def rmsnorm(x, w, *, tm=512):
    s, d = x.shape
    tm = min(tm, s)
    return pl.pallas_call(
        _rmsnorm_body,
        grid=(s // tm,),
        in_specs=[pl.BlockSpec((tm, d), lambda i: (i, 0)), pl.BlockSpec((1, d), lambda i: (0, 0))],
        out_specs=pl.BlockSpec((tm, d), lambda i: (i, 0)),
        out_shape=jax.ShapeDtypeStruct((s, d), F32),
        compiler_params=_cparams(("arbitrary",)), name="final_rmsnorm",
    )(x, w.reshape(1, d))


FFN_HALO = 16


def _ffn_up_body(x_ref, xh_ref, nw_ref, wg_ref, wu_ref, cw_ref, o_ref, xn_ref, xnh_ref, ebuf, *, row_chunk):
    tm = x_ref.shape[0]
    i = pl.program_id(0)

    def normed(x):
        ms = jnp.mean(x * x, axis=-1, keepdims=True)
        return (x * lax.rsqrt(ms + RMS_EPS) * nw_ref[...]).astype(BF16)

    @pl.when(pl.program_id(1) == 0)
    def _():
        def chunk(c, carry):
            r0 = pl.multiple_of(c * row_chunk, row_chunk)
            xn_ref[pl.ds(r0, row_chunk), :] = normed(x_ref[pl.ds(r0, row_chunk), :])
            return carry
        lax.fori_loop(0, tm // row_chunk, chunk, 0)
        xnh_ref[...] = normed(xh_ref[...])

    gate_h = jnp.dot(xnh_ref[...], wg_ref[...], preferred_element_type=F32)
    ebuf[0:FFN_HALO, :] = jnp.where(i > 0, gate_h, 0.0)
    ebuf[FFN_HALO:, :] = jnp.dot(xn_ref[...], wg_ref[...], preferred_element_type=F32)
    up = jnp.dot(xn_ref[...], wu_ref[...], preferred_element_type=F32)
    base = FFN_HALO - (FFN_CONV_K - 1)
    acc = jnp.zeros(o_ref.shape, F32)
    for t in range(FFN_CONV_K):
        acc = acc + cw_ref[t:t + 1, :] * ebuf[base + t: base + t + tm, :]
    o_ref[...] = (_silu(acc) * up).astype(o_ref.dtype)


def ffn_up_act(x, nw, w_up, conv_w, layer, *, tm=1024, tn=512):
    s, d = x.shape
    tm = min(tm, s)
    nj = D_FF // tn
    hb = tm // FFN_HALO
    halo_idx = lambda i: jnp.maximum(i * hb - 1, 0)
    cw = jnp.zeros((8, D_FF), F32).at[:FFN_CONV_K].set(conv_w)
    return pl.pallas_call(
        functools.partial(_ffn_up_body, row_chunk=min(128, tm)),
        grid=(s // tm, nj),
        in_specs=[pl.BlockSpec((tm, d), lambda i, j: (i, 0)),
                  pl.BlockSpec((FFN_HALO, d), lambda i, j: (halo_idx(i), 0)),
                  pl.BlockSpec((1, d), lambda i, j: (0, 0)),
                  pl.BlockSpec((None, d, tn), lambda i, j: (layer, 0, j)),
                  pl.BlockSpec((None, d, tn), lambda i, j: (layer, 0, nj + j)),
                  pl.BlockSpec((8, tn), lambda i, j: (0, j))],
        out_specs=pl.BlockSpec((tm, tn), lambda i, j: (i, j)),
        out_shape=jax.ShapeDtypeStruct((s, D_FF), BF16),
        scratch_shapes=[pltpu.VMEM((tm, d), BF16), pltpu.VMEM((FFN_HALO, d), BF16),
                        pltpu.VMEM((FFN_HALO + tm, tn), F32)],
        compiler_params=_cparams(("arbitrary", "arbitrary")), name="ffn_up_act",
    )(x, x, nw.reshape(1, d), w_up, w_up, cw)


def kernel(x, positions, norm1_w, w_in, gate_b, conv_dw_w, conv_dw_b, conv_ln_w, conv_ln_b, conv_out_w,
           ssd_conv_w, ssd_conv_b, ssd_dt_bias, ssd_a_log, ssd_d, ssd_norm_w, ssd_out_w,
           da_lambda_q1, da_lambda_k1, da_lambda_q2, da_lambda_k2, da_subln_w, da_out_w,
           w_o, norm2_w, ffn_up_w, ffn_dw_w, ffn_down_w, final_norm_w):
    bsz, s_len, d = x.shape
    assert bsz == 1 and d == D_MODEL
    xc = x.reshape(s_len, d)
    pos = positions.reshape(s_len)
    w_in_t = jnp.swapaxes(w_in, 1, 2).astype(BF16)
    conv_out_bf, ssd_out_bf, da_out_bf = conv_out_w.astype(BF16), ssd_out_w.astype(BF16), da_out_w.astype(BF16)
    w_o_bf, ffn_up_bf, ffn_down_bf = w_o.astype(BF16), ffn_up_w.astype(BF16), ffn_down_w.astype(BF16)
    for l in range(DEPTH):
        lambda_init = 0.8 - 0.6 * math.exp(-0.3 * l)
        u, dt_raw = in_proj(xc, norm1_w[l], w_in_t, l)
        h_a = conformer_pre(u, conv_dw_w[l], conv_dw_b[l], conv_ln_w[l], conv_ln_b[l])
        h_b = ssd_pre(u, dt_raw, ssd_conv_w[l], ssd_conv_b[l], ssd_dt_bias[l], ssd_a_log[l],
                      ssd_d[l], ssd_norm_w[l])
        q_t, k_r, v_t, k_norm = rope_prep(u, pos)
        h_c = diff_attention(q_t, k_r, v_t, k_norm, da_lambda_q1[l], da_lambda_k1[l], da_lambda_q2[l],
                             da_lambda_k2[l], da_subln_w[l], lambda_init)
        merged = gated_merge(h_a, h_b, h_c, conv_out_bf, ssd_out_bf, da_out_bf, u, gate_b[l], l)
        xc = residual_matmul(xc, merged, w_o_bf, l, tm=512, tn=d, name="wo_residual")
        act = ffn_up_act(xc, norm2_w[l], ffn_up_bf, ffn_dw_w[l], l)
        xc = residual_matmul(xc, act, ffn_down_bf, l, tm=1024, tn=256, name="ffn_down")
    return rmsnorm(xc, final_norm_w).reshape(bsz, s_len, d)
```

```python
import functools
import math

import jax
import jax.numpy as jnp
from jax import lax
from jax.experimental import pallas as pl
from jax.experimental.pallas import tpu as pltpu

F32 = jnp.float32
BF16 = jnp.bfloat16

D_MODEL = 2048
DEPTH = 2
CONV_CH = 1024
CONV_K = 31
SSD_HEADS = 32
SSD_HEADDIM = 64
SSD_INNER = SSD_HEADS * SSD_HEADDIM
SSD_GROUPS = 4
SSD_STATE = 128
SSD_CONV_K = 4
SSD_CHUNK = 128
SSD_GROUP_W = SSD_INNER // SSD_GROUPS
DA_HEADS = 8
DA_HEAD_DIM = 64
DA_V_DIM = 128
ROPE_DIM = 16
ROPE_THETA = 500000.0
D_FF = 5632
FFN_CONV_K = 3
RMS_EPS = 1e-6
LN_EPS = 1e-5
NEG_INF = -1e30

C_CONV_A = 0
C_CONV_G = 1024
C_Z = 2048
C_X = 4096
C_B = 6144
C_C = 6656
C_Q = 7168
C_K = 8192
C_V = 9216
C_GATE = 10240
U_WIDTH = 16384
DT_OFF = 7168
DT_PAD = 128

LANES = 128
VMEM_LIMIT = 56 * 1024 * 1024


def _cparams(sem, flags=None):
    return pltpu.CompilerParams(dimension_semantics=sem, vmem_limit_bytes=VMEM_LIMIT, flags=flags)


_NT = (((1,), (1,)), ((), ()))


def _in_proj_body(x_ref, nw_ref, w_ref, ws_ref, o_ref, os_ref, xn_ref, *, row_chunk):
    tm = x_ref.shape[0]

    @pl.when(pl.program_id(1) == 0)
    def _():
        def chunk(c, carry):
            r0 = pl.multiple_of(c * row_chunk, row_chunk)
            x = x_ref[pl.ds(r0, row_chunk), :]
            ms = jnp.mean(x * x, axis=-1, keepdims=True)
            xn_ref[pl.ds(r0, row_chunk), :] = (x * lax.rsqrt(ms + RMS_EPS) * nw_ref[...]).astype(BF16)
            return carry
        lax.fori_loop(0, tm // row_chunk, chunk, 0)
        os_ref[...] = lax.dot_general(xn_ref[...], ws_ref[...], _NT, preferred_element_type=F32)

    o_ref[...] = lax.dot_general(xn_ref[...], w_ref[0], _NT, preferred_element_type=F32).astype(o_ref.dtype)


def in_proj(x, nw, w_t, layer, *, tm=1024, tn=1024):
    s, d = x.shape
    tm = min(tm, s)
    n_blocks = U_WIDTH // tn
    row0 = lambda j: pl.multiple_of(j * tn + jnp.where(j * tn >= DT_OFF, SSD_HEADS, 0), SSD_HEADS)
    return pl.pallas_call(
        functools.partial(_in_proj_body, row_chunk=min(128, tm)),
        grid=(s // tm, n_blocks),
        in_specs=[pl.BlockSpec((tm, d), lambda i, j: (i, 0)),
                  pl.BlockSpec((1, d), lambda i, j: (0, 0)),
                  pl.BlockSpec((pl.Element(1), pl.Element(tn), pl.Element(d)), lambda i, j: (layer, row0(j), 0)),
                  pl.BlockSpec((None, DT_PAD, d), lambda i, j: (layer, DT_OFF // DT_PAD, 0))],
        out_specs=[pl.BlockSpec((tm, tn), lambda i, j: (i, j)),
                   pl.BlockSpec((tm, DT_PAD), lambda i, j: (i, 0))],
        out_shape=[jax.ShapeDtypeStruct((s, U_WIDTH), BF16), jax.ShapeDtypeStruct((s, DT_PAD), F32)],
        scratch_shapes=[pltpu.VMEM((tm, d), BF16)],
        compiler_params=_cparams(("arbitrary", "arbitrary")), name="in_proj",
    )(x, nw.reshape(1, d), w_t, w_t)


CONV_HALO = 32
CONV_ROWS = 128


def _conformer_body(a_ref, g_ref, ah_ref, gh_ref, w_ref, b_ref, lnw_ref, lnb_ref, o_ref, hbuf, cbuf):
    ts = a_ref.shape[0]
    i = pl.program_id(0)
    halo = ah_ref[...].astype(F32) * jax.nn.sigmoid(gh_ref[...].astype(F32))
    hbuf[0:CONV_HALO, :] = jnp.where(i > 0, halo, 0.0)
    hbuf[CONV_HALO:, :] = a_ref[...].astype(F32) * jax.nn.sigmoid(g_ref[...].astype(F32))
    base = CONV_HALO - (CONV_K - 1)
    sub = 8
    rows = CONV_ROWS
    ext = rows + CONV_HALO
    for c in range(CONV_CH // LANES):
        cs = slice(c * LANES, (c + 1) * LANES)
        for r0 in range(0, ts, rows):
            slab = hbuf[r0:r0 + ext, cs]
            acc = jnp.zeros((rows, LANES), F32)
            for res in range(sub):
                taps = [k for k in range(CONV_K) if (base + k) % sub == res]
                if not taps:
                    continue
                rolled = slab if res == 0 else pltpu.roll(slab, ext - res, axis=0)
                for k in taps:
                    a = (base + k) // sub * sub
                    acc = acc + w_ref[k:k + 1, cs] * rolled[a:a + rows]
            cbuf[r0:r0 + rows, cs] = acc + b_ref[:, cs]
    y = cbuf[...]
    mu = jnp.mean(y, axis=-1, keepdims=True)
    yc = y - mu
    var = jnp.mean(yc * yc, axis=-1, keepdims=True)
    yn = yc * lax.rsqrt(var + LN_EPS) * lnw_ref[...] + lnb_ref[...]
    o_ref[...] = (yn * jax.nn.sigmoid(yn)).astype(o_ref.dtype)


def conformer_pre(u, w_dw, b_dw, ln_w, ln_b, *, ts=256):
    s = u.shape[0]
    ts = min(ts, s)
    hb = ts // CONV_HALO
    w_pad = jnp.zeros((32, CONV_CH), F32).at[:CONV_K].set(w_dw)
    row = lambda v: v.reshape(1, CONV_CH)
    halo_idx = lambda i: jnp.maximum(i * hb - 1, 0)
    return pl.pallas_call(
        _conformer_body,
        grid=(s // ts,),
        in_specs=[pl.BlockSpec((ts, CONV_CH), lambda i: (i, C_CONV_A // CONV_CH)),
                  pl.BlockSpec((ts, CONV_CH), lambda i: (i, C_CONV_G // CONV_CH)),
                  pl.BlockSpec((CONV_HALO, CONV_CH), lambda i: (halo_idx(i), C_CONV_A // CONV_CH)),
                  pl.BlockSpec((CONV_HALO, CONV_CH), lambda i: (halo_idx(i), C_CONV_G // CONV_CH)),
                  pl.BlockSpec((32, CONV_CH), lambda i: (0, 0)),
                  pl.BlockSpec((1, CONV_CH), lambda i: (0, 0)),
                  pl.BlockSpec((1, CONV_CH), lambda i: (0, 0)),
                  pl.BlockSpec((1, CONV_CH), lambda i: (0, 0))],
        out_specs=pl.BlockSpec((ts, CONV_CH), lambda i: (i, 0)),
        out_shape=jax.ShapeDtypeStruct((s, CONV_CH), BF16),
        scratch_shapes=[pltpu.VMEM((CONV_HALO + ts, CONV_CH), F32), pltpu.VMEM((ts, CONV_CH), F32)],
        compiler_params=_cparams(("arbitrary",)), name="conformer_pre",
    )(u, u, u, u, w_pad, row(b_dw), row(ln_w), row(ln_b))


SSD_HALO = 8
SSD_CONV_ROWS = 64
SSD_CONV_LANES = 256


def _split3(v):
    hi = v.astype(BF16)
    r1 = v - hi.astype(F32)
    mid = r1.astype(BF16)
    lo = (r1 - mid.astype(F32)).astype(BF16)
    return hi, mid, lo


def _dot01_left(m01, v):
    hi, mid, lo = _split3(v)
    d = lambda p: jnp.dot(m01, p, preferred_element_type=F32)
    return d(hi) + d(mid) + d(lo)


def _softplus(x):
    return jnp.maximum(x, 0.0) + jnp.log1p(jnp.exp(-jnp.abs(x)))


def _silu(x):
    return x * jax.nn.sigmoid(x)


def _ssd_body(x_ref, b_ref, c_ref, z_ref, xh_ref, bh_ref, ch_ref, dt_ref,
              cwx_ref, cwb_ref, cwc_ref, cbx_ref, cbb_ref, cbc_ref,
              dtb_ref, alog_ref, dx_ref, nw_ref, ex_ref,
              o_ref, state, ebuf, xs_c, b_c, c_c, ybuf, dt_s, cum_s, cx_s, dtx_s):
    ts = x_ref.shape[0]
    L = SSD_CHUNK
    i = pl.program_id(0)

    @pl.when(i == 0)
    def _():
        state[...] = jnp.zeros_like(state)

    def conv(in_ref, halo_ref, w_ref, bias_ref, out_ref):
        width = in_ref.shape[1]
        ebuf[0:SSD_HALO, 0:width] = jnp.where(i > 0, halo_ref[...].astype(F32), 0.0)
        ebuf[SSD_HALO:, 0:width] = in_ref[...].astype(F32)
        base = SSD_HALO - (SSD_CONV_K - 1)
        rb, cw = SSD_CONV_ROWS, SSD_CONV_LANES
        ext = rb + SSD_HALO

        def row_block(bi, carry):
            r0 = pl.multiple_of(bi * rb, rb)
            for c in range(width // cw):
                cs = slice(c * cw, (c + 1) * cw)
                slab = ebuf[pl.ds(r0, ext), cs]
                acc = bias_ref[:, cs]
                for k in range(SSD_CONV_K):
                    a, res = (base + k) // 8 * 8, (base + k) % 8
                    rolled = slab if res == 0 else pltpu.roll(slab, ext - res, axis=0)
                    acc = acc + w_ref[k:k + 1, cs] * rolled[a:a + rb]
                out_ref[pl.ds(r0, rb), cs] = _silu(acc)
            return carry

        lax.fori_loop(0, ts // rb, row_block, 0)

    conv(x_ref, xh_ref, cwx_ref, cbx_ref, xs_c)
    conv(b_ref, bh_ref, cwb_ref, cbb_ref, b_c)
    conv(c_ref, ch_ref, cwc_ref, cbc_ref, c_c)

    a_row = -jnp.exp(alog_ref[...])
    rr = lax.broadcasted_iota(jnp.int32, (L, L), 0)
    cc = lax.broadcasted_iota(jnp.int32, (L, L), 1)
    tri = rr >= cc
    tri01 = jnp.where(tri, 1.0, 0.0).astype(BF16)
    lane_lo = lax.broadcasted_iota(jnp.int32, (L, LANES), 1) < SSD_HEADDIM

    dt_all = _softplus(dt_ref[...] + dtb_ref[...])
    dt_s[...] = dt_all
    for ci in range(ts // L):
        cum_s[ci * L:(ci + 1) * L, :] = _dot01_left(tri01, dt_all[ci * L:(ci + 1) * L] * a_row)
    for src, dst in ((cum_s, cx_s), (dt_s, dtx_s)):
        dst[...] = jnp.dot(jnp.concatenate(_split3(src[...]), axis=1), ex_ref[...],
                           preferred_element_type=F32)

    def chunk(ci, carry):
        r0 = pl.multiple_of(ci * L, L)
        rows = pl.ds(r0, L)
        dt = dt_s[rows, :]
        cum = cum_s[rows, :]
        cum_t = cum.T
        dt_t = dt.T
        cum_x = cx_s[rows, :]
        dt_x = dtx_s[rows, :]
        cum_last = cum_x[L - 1:L, :]
        e_x = jnp.exp(cum_x)
        xs = xs_c[rows, :]
        xd = (xs * jnp.exp(cum_last - cum_x) * dt_x).astype(BF16)
        e_last = jnp.exp(cum_last)
        for g in range(SSD_GROUPS):
            gs = slice(g * SSD_GROUP_W, (g + 1) * SSD_GROUP_W)
            ns = slice(g * SSD_STATE, (g + 1) * SSD_STATE)
            bg = b_c[rows, ns]
            cg = c_c[rows, ns].astype(BF16)
            cb = lax.dot_general(cg, bg.astype(BF16), (((1,), (1,)), ((), ())),
                                 preferred_element_type=F32)
            st = state[:, gs]
            y_off = jnp.dot(cg, st.astype(BF16), preferred_element_type=F32) * e_x[:, gs]
            state[:, gs] = e_last[:, gs] * st + jnp.dot(bg.T.astype(BF16), xd[:, gs],
                                                        preferred_element_type=F32)
            for pr in range(SSD_GROUP_W // LANES):
                h0 = g * (SSD_HEADS // SSD_GROUPS) + 2 * pr
                ps = slice(h0 * SSD_HEADDIM, h0 * SSD_HEADDIM + LANES)
                x_pair = xs[:, ps].astype(BF16)
                ys = []
                for h in (h0, h0 + 1):
                    seg = cum[:, h:h + 1] - cum_t[h:h + 1, :]
                    decay = jnp.where(tri, jnp.exp(jnp.minimum(seg, 0.0)), 0.0)
                    wts = (cb * decay * dt_t[h:h + 1, :]).astype(BF16)
                    ys.append(jnp.dot(wts, x_pair, preferred_element_type=F32))
                y_pair = jnp.where(lane_lo, ys[0], ys[1])
                ybuf[:, ps] = y_pair + y_off[:, pr * LANES:(pr + 1) * LANES]
        y = ybuf[...] + xs * dx_ref[...]
        y = y * _silu(z_ref[rows, :].astype(F32))
        for g in range(SSD_GROUPS):
            gs = slice(g * SSD_GROUP_W, (g + 1) * SSD_GROUP_W)
            yg = y[:, gs]
            ms = jnp.mean(yg * yg, axis=-1, keepdims=True)
            o_ref[rows, gs] = (yg * lax.rsqrt(ms + RMS_EPS) * nw_ref[:, gs]).astype(o_ref.dtype)
        return carry

    lax.fori_loop(0, ts // L, chunk, 0)


def ssd_pre(u, dt_raw, conv_w, conv_b, dt_bias, a_log, d_skip, norm_w, *, ts=512):
    s = u.shape[0]
    ts = min(ts, s)
    hb = ts // SSD_HALO
    halo_idx = lambda i: jnp.maximum(i * hb - 1, 0)
    cw = jnp.zeros((8, conv_w.shape[1]), F32).at[:SSD_CONV_K].set(conv_w)
    n_bc = SSD_GROUPS * SSD_STATE
    cwx, cwb, cwc = cw[:, :SSD_INNER], cw[:, SSD_INNER:SSD_INNER + n_bc], cw[:, SSD_INNER + n_bc:]
    cb2 = conv_b.reshape(1, -1)
    cbx, cbb, cbc = cb2[:, :SSD_INNER], cb2[:, SSD_INNER:SSD_INNER + n_bc], cb2[:, SSD_INNER + n_bc:]
    pad_h = lambda v: jnp.zeros((1, DT_PAD), F32).at[0, :SSD_HEADS].set(v)
    dx = jnp.repeat(d_skip, SSD_HEADDIM).reshape(1, SSD_INNER)
    ex = (jnp.arange(SSD_INNER)[None, :] // SSD_HEADDIM == jnp.arange(DT_PAD)[:, None]).astype(BF16)
    ex = jnp.concatenate([ex, ex, ex], axis=0)
    const = lambda shape: pl.BlockSpec(shape, lambda i: (0, 0))
    return pl.pallas_call(
        _ssd_body,
        grid=(s // ts,),
        in_specs=[pl.BlockSpec((ts, SSD_INNER), lambda i: (i, C_X // SSD_INNER)),
                  pl.BlockSpec((ts, n_bc), lambda i: (i, C_B // n_bc)),
                  pl.BlockSpec((ts, n_bc), lambda i: (i, C_C // n_bc)),
                  pl.BlockSpec((ts, SSD_INNER), lambda i: (i, C_Z // SSD_INNER)),
                  pl.BlockSpec((SSD_HALO, SSD_INNER), lambda i: (halo_idx(i), C_X // SSD_INNER)),
                  pl.BlockSpec((SSD_HALO, n_bc), lambda i: (halo_idx(i), C_B // n_bc)),
                  pl.BlockSpec((SSD_HALO, n_bc), lambda i: (halo_idx(i), C_C // n_bc)),
                  pl.BlockSpec((ts, DT_PAD), lambda i: (i, 0)),
                  const((8, SSD_INNER)), const((8, n_bc)), const((8, n_bc)),
                  const((1, SSD_INNER)), const((1, n_bc)), const((1, n_bc)),
                  const((1, DT_PAD)), const((1, DT_PAD)), const((1, SSD_INNER)), const((1, SSD_INNER)),
                  const((3 * DT_PAD, SSD_INNER))],
        out_specs=pl.BlockSpec((ts, SSD_INNER), lambda i: (i, 0)),
        out_shape=jax.ShapeDtypeStruct((s, SSD_INNER), BF16),
        scratch_shapes=[pltpu.VMEM((SSD_STATE, SSD_INNER), F32),
                        pltpu.VMEM((SSD_HALO + ts, SSD_INNER), F32),
                        pltpu.VMEM((ts, SSD_INNER), F32),
                        pltpu.VMEM((ts, n_bc), F32),
                        pltpu.VMEM((ts, n_bc), F32),
                        pltpu.VMEM((SSD_CHUNK, SSD_INNER), F32),
                        pltpu.VMEM((ts, DT_PAD), F32), pltpu.VMEM((ts, DT_PAD), F32),
                        pltpu.VMEM((ts, SSD_INNER), F32), pltpu.VMEM((ts, SSD_INNER), F32)],
        compiler_params=_cparams(("arbitrary",)), name="ssd_pre",
    )(u, u, u, u, u, u, u, dt_raw, cwx, cwb, cwc, cbx, cbb, cbc,
      pad_h(dt_bias), pad_h(a_log), dx, norm_w.reshape(1, SSD_INNER), ex)


LOG2E = 1.4426950408889634
ATT_TQ = 1024
ATT_TK = ATT_TQ // 2
ATT_V_ROWS = DA_V_DIM + 16
ATT_NORM_SLACK = 1.02
ATT_MIN_ROW_SUM = 2.0 ** -64


def _rope_body(q_ref, k_ref, v_ref, pos_ref, inv_ref, qo_ref, ko_ref, vo_ref, kn_ref):
    ts = q_ref.shape[0]
    li = lax.broadcasted_iota(jnp.int32, (LANES, LANES), 0) // DA_HEAD_DIM
    lj = lax.broadcasted_iota(jnp.int32, (LANES, LANES), 1) // DA_HEAD_DIM
    same_comp = jnp.where(li == lj, 1.0, 0.0).astype(BF16)
    lane = lax.broadcasted_iota(jnp.int32, (ts, LANES), 1) % DA_HEAD_DIM
    half = ROPE_DIM // 2
    ang = pos_ref[...].astype(F32) * inv_ref[...]
    cos = jnp.cos(ang)
    sin = jnp.sin(ang)
    c_all = jnp.where(lane < ROPE_DIM, cos, 1.0)
    s_lo = jnp.where(lane < half, -sin, 0.0)
    s_hi = jnp.where((lane >= half) & (lane < ROPE_DIM), sin, 0.0)
    q_scale = DA_HEAD_DIM ** -0.5 * LOG2E

    def rotate(t):
        up = pltpu.roll(t, LANES - half, axis=1)
        dn = pltpu.roll(t, half, axis=1)
        return t * c_all + up * s_lo + dn * s_hi

    for c in range(q_ref.shape[1] // LANES):
        cs = slice(c * LANES, (c + 1) * LANES)
        kr = rotate(k_ref[:, cs].astype(F32)).astype(ko_ref.dtype)
        ko_ref[:, cs] = kr
        kf = kr.astype(F32)
        norm2 = jnp.dot((kf * kf).astype(BF16), same_comp, preferred_element_type=F32)
        kn_ref[c:c + 1, :] = jnp.sqrt(jnp.max(norm2, axis=0, keepdims=True)) * ATT_NORM_SLACK
        qo_ref[cs, :] = (rotate(q_ref[:, cs].astype(F32)) * q_scale).T.astype(qo_ref.dtype)
        vo_ref[c, 0:DA_V_DIM, :] = v_ref[:, cs].astype(F32).T.astype(vo_ref.dtype)
        extra = lax.broadcasted_iota(jnp.int32, (ATT_V_ROWS - DA_V_DIM, ts), 0) == 0
        vo_ref[c, DA_V_DIM:ATT_V_ROWS, :] = jnp.where(extra, 1.0, 0.0).astype(vo_ref.dtype)


def rope_prep(u, positions, *, ts=ATT_TK):
    s = u.shape[0]
    ts = min(ts, s)
    w = DA_HEADS * 2 * DA_HEAD_DIM
    inv = 1.0 / (ROPE_THETA ** (jnp.arange(0, ROPE_DIM, 2, dtype=F32) / ROPE_DIM))
    lane = jnp.arange(LANES) % DA_HEAD_DIM
    inv_lane = jnp.where(lane < ROPE_DIM, inv[lane % (ROPE_DIM // 2)], 0.0).reshape(1, LANES)
    return pl.pallas_call(
        _rope_body,
        grid=(s // ts,),
        in_specs=[pl.BlockSpec((ts, w), lambda i: (i, C_Q // w)),
                  pl.BlockSpec((ts, w), lambda i: (i, C_K // w)),
                  pl.BlockSpec((ts, w), lambda i: (i, C_V // w)),
                  pl.BlockSpec((ts, 1), lambda i: (i, 0)),
                  pl.BlockSpec((1, LANES), lambda i: (0, 0))],
        out_specs=[pl.BlockSpec((w, ts), lambda i: (0, i)), pl.BlockSpec((ts, w), lambda i: (i, 0)),
                   pl.BlockSpec((DA_HEADS, None, ATT_V_ROWS, ts), lambda i: (0, i, 0, 0)),
                   pl.BlockSpec((None, DA_HEADS, LANES), lambda i: (i, 0, 0))],
        out_shape=[jax.ShapeDtypeStruct((w, s), BF16), jax.ShapeDtypeStruct((s, w), BF16),
                   jax.ShapeDtypeStruct((DA_HEADS, s // ts, ATT_V_ROWS, ts), BF16),
                   jax.ShapeDtypeStruct((s // ts, DA_HEADS, LANES), F32)],
        compiler_params=_cparams(("arbitrary",)), name="rope_prep",
    )(u, u, u, positions.reshape(s, 1), inv_lane)


def _attn_body(qt_ref, k_ref, vt_ref, kn_ref, lq1_ref, lk1_ref, lq2_ref, lk2_ref, sw_ref, o_ref,
               m_ref, acc_ref, sa_ref, sb_ref, ma_ref, mb_ref, *, lambda_init):
    tq = qt_ref.shape[1]
    tk = vt_ref.shape[2]
    assert tq == 2 * tk
    i = pl.program_id(1)
    row = lax.broadcasted_iota(jnp.int32, (LANES, tq), 0)
    qt = qt_ref[...]
    zero = jnp.zeros_like(qt)
    qc = (jnp.where(row < DA_HEAD_DIM, qt, zero), jnp.where(row >= DA_HEAD_DIM, qt, zero))
    m_ref[...] = jnp.full(m_ref.shape, NEG_INF, F32)
    acc_ref[...] = jnp.zeros(acc_ref.shape, F32)

    def scores(j, s_ref, mc_ref):
        kb = k_ref[pl.ds(pl.multiple_of(j * tk, tk), tk), :]
        for c in range(2):
            s = jnp.dot(kb, qc[c], preferred_element_type=F32)
            s_ref[c] = s
            mc_ref[c] = jnp.max(s, axis=0, keepdims=True)

    def consume(j, s_ref, mc_ref, masked):
        vtb = vt_ref[j]
        for c in range(2):
            s = s_ref[c]
            if masked:
                kk = lax.broadcasted_iota(jnp.int32, (tk, tq), 0) + (j * tk - i * tq)
                qq = lax.broadcasted_iota(jnp.int32, (tk, tq), 1)
                s = jnp.where(kk <= qq, s, NEG_INF)
                m_cur = jnp.max(s, axis=0, keepdims=True)
            else:
                m_cur = mc_ref[c]
            m_prev = m_ref[c]
            m_new = jnp.maximum(m_prev, m_cur)
            alpha = jnp.exp2(m_prev - m_new)
            p = jnp.exp2(s - m_new).astype(BF16)
            acc_ref[c] = alpha * acc_ref[c] + jnp.dot(vtb, p, preferred_element_type=F32)
            m_ref[c] = m_new

    def pair(t, carry):
        j0 = 2 * t
        scores(j0 + 1, sb_ref, mb_ref)
        consume(j0, sa_ref, ma_ref, False)
        scores(j0 + 2, sa_ref, ma_ref)
        consume(j0 + 1, sb_ref, mb_ref, False)
        return carry

    qn = [jnp.sqrt(jnp.sum(jnp.square(qc[c].astype(F32)), axis=0, keepdims=True)) for c in range(2)]

    def stream(j, lo=0, hi=None, triangular=False):
        hi = tq if hi is None else hi
        qs = slice(lo, hi)
        kb = k_ref[pl.ds(pl.multiple_of(j * tk, tk), tk), :]
        vtb = vt_ref[j]
        kn = kn_ref[pl.ds(j, 1), :]
        for c in range(2):
            s = jnp.dot(kb, qc[c][:, qs], preferred_element_type=F32)
            if triangular:
                kk = lax.broadcasted_iota(jnp.int32, s.shape, 0)
                qq = lax.broadcasted_iota(jnp.int32, s.shape, 1)
                s = jnp.where(kk <= qq, s, NEG_INF)
            m_prev = m_ref[c, :, qs]
            m_new = jnp.maximum(m_prev, qn[c][:, qs] * kn[:, c * DA_HEAD_DIM:c * DA_HEAD_DIM + 1])
            alpha = jnp.exp2(m_prev - m_new)
            p = jnp.exp2(s - m_new).astype(BF16)
            acc_ref[c, :, qs] = alpha * acc_ref[c, :, qs] + jnp.dot(vtb, p, preferred_element_type=F32)
            m_ref[c, :, qs] = m_new

    def stream_pair(t, carry):
        stream(2 * t)
        stream(2 * t + 1)
        return carry

    lax.fori_loop(0, i, stream_pair, 0)
    stream(2 * i, 0, tk, triangular=True)
    stream(2 * i, tk, tq)
    stream(2 * i + 1, tk, tq, triangular=True)

    row_sums = jnp.minimum(acc_ref[0, DA_V_DIM:DA_V_DIM + 1, :], acc_ref[1, DA_V_DIM:DA_V_DIM + 1, :])
    healthy = jnp.min(row_sums) >= ATT_MIN_ROW_SUM

    @pl.when(jnp.logical_not(healthy))
    def _():
        m_ref[...] = jnp.full(m_ref.shape, NEG_INF, F32)
        acc_ref[...] = jnp.zeros(acc_ref.shape, F32)
        scores(0, sa_ref, ma_ref)
        lax.fori_loop(0, i, pair, 0)
        scores(2 * i + 1, sb_ref, mb_ref)
        consume(2 * i, sa_ref, ma_ref, True)
        consume(2 * i + 1, sb_ref, mb_ref, True)

    lam = (jnp.exp(jnp.sum(lq1_ref[...] * lk1_ref[...], axis=-1, keepdims=True))
           - jnp.exp(jnp.sum(lq2_ref[...] * lk2_ref[...], axis=-1, keepdims=True)) + lambda_init)
    nd = DA_V_DIM
    ot = (acc_ref[0, 0:nd, :] / acc_ref[0, nd:nd + 1, :]
          - lam * (acc_ref[1, 0:nd, :] / acc_ref[1, nd:nd + 1, :]))
    ms = jnp.mean(ot * ot, axis=0, keepdims=True)
    ot = ot * (lax.rsqrt(ms + LN_EPS) * (1.0 - lambda_init))
    o_ref[...] = (ot.T * sw_ref[...]).astype(o_ref.dtype)


def diff_attention(q_t, k_r, v_t, k_norm, lq1, lk1, lq2, lk2, subln_w, lambda_init, *, tq=ATT_TQ):
    s = k_r.shape[0]
    nkb, tk = v_t.shape[1], v_t.shape[3]
    tq = min(tq, s)
    row = lambda v: v.reshape(1, -1)
    const = lambda n: pl.BlockSpec((1, n), lambda h, i: (0, 0))
    return pl.pallas_call(
        functools.partial(_attn_body, lambda_init=lambda_init),
        grid=(DA_HEADS, s // tq),
        in_specs=[pl.BlockSpec((LANES, tq), lambda h, i: (h, i)),
                  pl.BlockSpec((s, LANES), lambda h, i: (0, h)),
                  pl.BlockSpec((None, nkb, ATT_V_ROWS, tk), lambda h, i: (h, 0, 0, 0)),
                  pl.BlockSpec((None, nkb, LANES), lambda h, i: (h, 0, 0)),
                  const(DA_HEAD_DIM), const(DA_HEAD_DIM), const(DA_HEAD_DIM), const(DA_HEAD_DIM),
                  const(DA_V_DIM)],
        out_specs=pl.BlockSpec((tq, LANES), lambda h, i: (i, h)),
        out_shape=jax.ShapeDtypeStruct((s, DA_HEADS * DA_V_DIM), BF16),
        scratch_shapes=[pltpu.VMEM((2, 1, tq), F32),
                        pltpu.VMEM((2, ATT_V_ROWS, tq), F32),
                        pltpu.VMEM((2, tk, tq), F32), pltpu.VMEM((2, tk, tq), F32),
                        pltpu.VMEM((2, 1, tq), F32), pltpu.VMEM((2, 1, tq), F32)],
        compiler_params=_cparams(("arbitrary", "arbitrary")), name="diff_attention",
    )(q_t, k_r, v_t, jnp.swapaxes(k_norm, 0, 1), row(lq1), row(lk1), row(lq2), row(lk2), row(subln_w))


def _merge_body(ha_ref, hb_ref, hc_ref, wa_ref, wb_ref, wc_ref, ga_ref, gb_ref, gc_ref, bias_ref, o_ref):
    def gated(h_ref, w_ref, g_ref, bi):
        y = jnp.dot(h_ref[...], w_ref[...], preferred_element_type=F32)
        return jax.nn.sigmoid(g_ref[...].astype(F32) + bias_ref[bi:bi + 1, :]) * y

    m = gated(ha_ref, wa_ref, ga_ref, 0) + gated(hb_ref, wb_ref, gb_ref, 1) + gated(hc_ref, wc_ref, gc_ref, 2)
    o_ref[...] = m.astype(o_ref.dtype)


def gated_merge(h_a, h_b, h_c, w_a, w_b, w_c, u, gate_b, layer, *, tm=1024, tn=512):
    s = h_a.shape[0]
    d = w_a.shape[2]
    tm = min(tm, s)
    wspec = lambda w: pl.BlockSpec((None, w.shape[1], tn), lambda i, j: (layer, 0, j))
    gblk = C_GATE // tn
    nblk = d // tn
    return pl.pallas_call(
        _merge_body,
        grid=(s // tm, d // tn),
        in_specs=[pl.BlockSpec((tm, h_a.shape[1]), lambda i, j: (i, 0)),
                  pl.BlockSpec((tm, h_b.shape[1]), lambda i, j: (i, 0)),
                  pl.BlockSpec((tm, h_c.shape[1]), lambda i, j: (i, 0)),
                  wspec(w_a), wspec(w_b), wspec(w_c),
                  pl.BlockSpec((tm, tn), lambda i, j: (i, gblk + j)),
                  pl.BlockSpec((tm, tn), lambda i, j: (i, gblk + nblk + j)),
                  pl.BlockSpec((tm, tn), lambda i, j: (i, gblk + 2 * nblk + j)),
                  pl.BlockSpec((3, tn), lambda i, j: (0, j))],
        out_specs=pl.BlockSpec((tm, tn), lambda i, j: (i, j)),
        out_shape=jax.ShapeDtypeStruct((s, d), BF16),
        compiler_params=_cparams(("arbitrary", "arbitrary")), name="gated_merge",
    )(h_a, h_b, h_c, w_a, w_b, w_c, u, u, u, gate_b)


def _residual_matmul_body(x_ref, a_ref, w_ref, o_ref):
    o_ref[...] = x_ref[...] + jnp.dot(a_ref[...], w_ref[...], preferred_element_type=F32)


def residual_matmul(x, a, w, layer, *, tm, tn, name):
    s, d = x.shape
    tm = min(tm, s)
    return pl.pallas_call(
        _residual_matmul_body,
        grid=(s // tm, d // tn),
        in_specs=[pl.BlockSpec((tm, tn), lambda i, j: (i, j)),
                  pl.BlockSpec((tm, a.shape[1]), lambda i, j: (i, 0)),
                  pl.BlockSpec((None, w.shape[1], tn), lambda i, j: (layer, 0, j))],
        out_specs=pl.BlockSpec((tm, tn), lambda i, j: (i, j)),
        out_shape=jax.ShapeDtypeStruct((s, d), F32),
        compiler_params=_cparams(("arbitrary", "arbitrary")), name=name,
    )(x, a, w)


def _rmsnorm_body(x_ref, w_ref, o_ref):
    x = x_ref[...]
    ms = jnp.mean(x * x, axis=-1, keepdims=True)
    o_ref[...] = x * lax.rsqrt(ms + RMS_EPS) * w_ref[...]


def rmsnorm(x, w, *, tm=512):
    s, d = x.shape
    tm = min(tm, s)
    return pl.pallas_call(
        _rmsnorm_body,
        grid=(s // tm,),
        in_specs=[pl.BlockSpec((tm, d), lambda i: (i, 0)), pl.BlockSpec((1, d), lambda i: (0, 0))],
        out_specs=pl.BlockSpec((tm, d), lambda i: (i, 0)),
        out_shape=jax.ShapeDtypeStruct((s, d), F32),
        compiler_params=_cparams(("arbitrary",)), name="final_rmsnorm",
    )(x, w.reshape(1, d))


FFN_HALO = 16


def _ffn_up_body(x_ref, xh_ref, nw_ref, wg_ref, wu_ref, cw_ref, o_ref, xn_ref, xnh_ref, ebuf, *, row_chunk):
    tm = x_ref.shape[0]
    i = pl.program_id(0)

    def normed(x):
        ms = jnp.mean(x * x, axis=-1, keepdims=True)
        return (x * lax.rsqrt(ms + RMS_EPS) * nw_ref[...]).astype(BF16)

    @pl.when(pl.program_id(1) == 0)
    def _():
        def chunk(c, carry):
            r0 = pl.multiple_of(c * row_chunk, row_chunk)
            xn_ref[pl.ds(r0, row_chunk), :] = normed(x_ref[pl.ds(r0, row_chunk), :])
            return carry
        lax.fori_loop(0, tm // row_chunk, chunk, 0)
        xnh_ref[...] = normed(xh_ref[...])

    gate_h = jnp.dot(xnh_ref[...], wg_ref[...], preferred_element_type=F32)
    ebuf[0:FFN_HALO, :] = jnp.where(i > 0, gate_h, 0.0)
    ebuf[FFN_HALO:, :] = jnp.dot(xn_ref[...], wg_ref[...], preferred_element_type=F32)
    up = jnp.dot(xn_ref[...], wu_ref[...], preferred_element_type=F32)
    base = FFN_HALO - (FFN_CONV_K - 1)
    acc = jnp.zeros(o_ref.shape, F32)
    for t in range(FFN_CONV_K):
        acc = acc + cw_ref[t:t + 1, :] * ebuf[base + t: base + t + tm, :]
    o_ref[...] = (_silu(acc) * up).astype(o_ref.dtype)


def ffn_up_act(x, nw, w_up, conv_w, layer, *, tm=1024, tn=512):
    s, d = x.shape
    tm = min(tm, s)
    nj = D_FF // tn
    hb = tm // FFN_HALO
    halo_idx = lambda i: jnp.maximum(i * hb - 1, 0)
    cw = jnp.zeros((8, D_FF), F32).at[:FFN_CONV_K].set(conv_w)
    return pl.pallas_call(
        functools.partial(_ffn_up_body, row_chunk=min(128, tm)),
        grid=(s // tm, nj),
        in_specs=[pl.BlockSpec((tm, d), lambda i, j: (i, 0)),
                  pl.BlockSpec((FFN_HALO, d), lambda i, j: (halo_idx(i), 0)),
                  pl.BlockSpec((1, d), lambda i, j: (0, 0)),
                  pl.BlockSpec((None, d, tn), lambda i, j: (layer, 0, j)),
                  pl.BlockSpec((None, d, tn), lambda i, j: (layer, 0, nj + j)),
                  pl.BlockSpec((8, tn), lambda i, j: (0, j))],
        out_specs=pl.BlockSpec((tm, tn), lambda i, j: (i, j)),
        out_shape=jax.ShapeDtypeStruct((s, D_FF), BF16),
        scratch_shapes=[pltpu.VMEM((tm, d), BF16), pltpu.VMEM((FFN_HALO, d), BF16),
                        pltpu.VMEM((FFN_HALO + tm, tn), F32)],
        compiler_params=_cparams(("arbitrary", "arbitrary")), name="ffn_up_act",
    )(x, x, nw.reshape(1, d), w_up, w_up, cw)


def kernel(x, positions, norm1_w, w_in, gate_b, conv_dw_w, conv_dw_b, conv_ln_w, conv_ln_b, conv_out_w,
           ssd_conv_w, ssd_conv_b, ssd_dt_bias, ssd_a_log, ssd_d, ssd_norm_w, ssd_out_w,
           da_lambda_q1, da_lambda_k1, da_lambda_q2, da_lambda_k2, da_subln_w, da_out_w,
           w_o, norm2_w, ffn_up_w, ffn_dw_w, ffn_down_w, final_norm_w):
    bsz, s_len, d = x.shape
    assert bsz == 1 and d == D_MODEL
    xc = x.reshape(s_len, d)
    pos = positions.reshape(s_len)
    w_in_t = jnp.swapaxes(w_in, 1, 2).astype(BF16)
    conv_out_bf, ssd_out_bf, da_out_bf = conv_out_w.astype(BF16), ssd_out_w.astype(BF16), da_out_w.astype(BF16)
    w_o_bf, ffn_up_bf, ffn_down_bf = w_o.astype(BF16), ffn_up_w.astype(BF16), ffn_down_w.astype(BF16)
    for l in range(DEPTH):
        lambda_init = 0.8 - 0.6 * math.exp(-0.3 * l)
        u, dt_raw = in_proj(xc, norm1_w[l], w_in_t, l)
        h_a = conformer_pre(u, conv_dw_w[l], conv_dw_b[l], conv_ln_w[l], conv_ln_b[l])
        h_b = ssd_pre(u, dt_raw, ssd_conv_w[l], ssd_conv_b[l], ssd_dt_bias[l], ssd_a_log[l],
                      ssd_d[l], ssd_norm_w[l])
        q_t, k_r, v_t, k_norm = rope_prep(u, pos)
        h_c = diff_attention(q_t, k_r, v_t, k_norm, da_lambda_q1[l], da_lambda_k1[l], da_lambda_q2[l],
                             da_lambda_k2[l], da_subln_w[l], lambda_init)
        merged = gated_merge(h_a, h_b, h_c, conv_out_bf, ssd_out_bf, da_out_bf, u, gate_b[l], l)
        xc = residual_matmul(xc, merged, w_o_bf, l, tm=512, tn=d, name="wo_residual")
        act = ffn_up_act(xc, norm2_w[l], ffn_up_bf, ffn_dw_w[l], l)
        xc = residual_matmul(xc, act, ffn_down_bf, l, tm=1024, tn=512, name="ffn_down")
    return rmsnorm(xc, final_norm_w).reshape(bsz, s_len, d)
```
